```python
import math
import jax, jax.numpy as jnp
from jax import lax
import numpy as np

D_MODEL = 4096
BATCH = 2
SEQ = 8192
DEPTH = 4

MIX_WIDTH = D_MODEL
BRANCH_WIDTH = MIX_WIDTH // 4
HEAD_DIM = 128
N_FOX_HEADS = BRANCH_WIDTH // HEAD_DIM
N_SB_HEADS = BRANCH_WIDTH // HEAD_DIM
SSM_GROUP = 16
N_SSM_GROUPS = BRANCH_WIDTH // SSM_GROUP
SSM_STATE = 64
LRU_BLOCKS = 16
LRU_BLOCK = BRANCH_WIDTH // LRU_BLOCKS
CONV_WIDTH = 4
LRU_C = 8.0
Q_BLOCK = 128
RMS_EPS = 1e-6
N_IN = 12 * BRANCH_WIDTH + N_FOX_HEADS

kernel_name = "hybrid_fox_sb_s5_rglru_parallel"


def rmsnorm(x, g):
    xf = x.astype(jnp.float32)
    y = xf * lax.rsqrt(jnp.mean(xf * xf, axis=-1, keepdims=True) + RMS_EPS)
    return (y * g.astype(jnp.float32)).astype(x.dtype)


def _split_heads(t, n_heads):
    b, l, _ = t.shape
    return t.reshape(b, l, n_heads, HEAD_DIM).transpose(0, 2, 1, 3)


def _merge_heads(t):
    b, h, l, d = t.shape
    return t.transpose(0, 2, 1, 3).reshape(b, l, h * d)


def _to_blocks(t):
    b, h, l = t.shape[:3]
    t = t.reshape((b, h, l // Q_BLOCK, Q_BLOCK) + t.shape[3:])
    return jnp.moveaxis(t, 2, 0)


def forgetting_attention(q, k, v, log_f):
    b, h, l, d = q.shape
    nb = l // Q_BLOCK
    cum_f = jnp.cumsum(log_f, axis=-1)
    k_pos = jnp.arange(l)
    scale = d ** -0.5

    def one_block(args):
        q_blk, f_blk, bi = args
        q_pos = bi * Q_BLOCK + jnp.arange(Q_BLOCK)
        s = jnp.einsum('bhqd,bhkd->bhqk', q_blk, k).astype(jnp.float32) * scale
        s = s + f_blk[..., :, None] - cum_f[..., None, :]
        mask = k_pos[None, :] <= q_pos[:, None]
        p = jax.nn.softmax(jnp.where(mask, s, -jnp.inf), axis=-1)
        return jnp.einsum('bhqk,bhkd->bhqd', p.astype(v.dtype), v)

    out = lax.map(one_block, (_to_blocks(q), _to_blocks(cum_f), jnp.arange(nb)))
    return jnp.moveaxis(out, 0, 2).reshape(b, h, l, d)


def stick_breaking_attention(q, k, v):
    b, h, l, d = q.shape
    nb = l // Q_BLOCK
    k_pos = jnp.arange(l)
    scale = d ** -0.5

    def one_block(args):
        q_blk, bi = args
        q_pos = bi * Q_BLOCK + jnp.arange(Q_BLOCK)
        z = jnp.einsum('bhqd,bhkd->bhqk', q_blk, k).astype(jnp.float32) * scale
        mask = k_pos[None, :] < q_pos[:, None]
        log_keep = jnp.where(mask, jax.nn.log_sigmoid(-z), 0.0)
        after = lax.cumsum(log_keep, axis=3, reverse=True) - log_keep
        w = jnp.where(mask, jnp.exp(jax.nn.log_sigmoid(z) + after), 0.0)
        return jnp.einsum('bhqk,bhkd->bhqd', w.astype(v.dtype), v)

    out = lax.map(one_block, (_to_blocks(q), jnp.arange(nb)))
    return jnp.moveaxis(out, 0, 2).reshape(b, h, l, d)


def _linear_scan(a, bx):
    def combine(left, right):
        a_l, b_l = left
        a_r, b_r = right
        return a_r * a_l, a_r * b_l + b_r
    _, hs = lax.associative_scan(combine, (a, bx), axis=1)
    return hs


def s5_mixer(u, a_re, a_im, log_dt, b_re, b_im, c_re, c_im, d_skip, w_glu, b_glu):
    bsz, l, w = u.shape
    uf = u.astype(jnp.float32).reshape(bsz, l, N_SSM_GROUPS, SSM_GROUP)
    dt = jnp.exp(log_dt.astype(jnp.float32))[:, None]
    lam = lax.complex(a_re.astype(jnp.float32), a_im.astype(jnp.float32))
    lam_bar = jnp.exp(lam * dt)
    b_c = lax.complex(b_re.astype(jnp.float32), b_im.astype(jnp.float32))
    b_bar = ((lam_bar - 1.0) / lam)[..., None] * b_c
    bu = jnp.einsum('gph,blgh->blgp', b_bar, uf.astype(jnp.complex64))
    states = _linear_scan(jnp.broadcast_to(lam_bar, bu.shape), bu)
    c_c = lax.complex(c_re.astype(jnp.float32), c_im.astype(jnp.float32))
    y = jnp.einsum('ghp,blgp->blgh', c_c, states).real
    y = y + d_skip.astype(jnp.float32).reshape(N_SSM_GROUPS, SSM_GROUP) * uf
    y = jax.nn.gelu(y.reshape(bsz, l, w))
    y = y * jax.nn.sigmoid(y @ w_glu.astype(jnp.float32) + b_glu.astype(jnp.float32))
    return y.astype(u.dtype)


def rglru_mixer(xb, conv_w, conv_b, w_r, b_r, w_i, b_i, lam):
    bsz, l, w = xb.shape
    xc = lax.conv_general_dilated(
        xb, conv_w[:, None, :], window_strides=(1,), padding=[(CONV_WIDTH - 1, 0)],
        dimension_numbers=('NWC', 'WIO', 'NWC'), feature_group_count=w) + conv_b
    xf = xc.astype(jnp.float32)
    xblk = xf.reshape(bsz, l, LRU_BLOCKS, LRU_BLOCK)
    r = jax.nn.sigmoid(jnp.einsum('blni,nio->blno', xblk, w_r.astype(jnp.float32)).reshape(bsz, l, w)
                       + b_r.astype(jnp.float32))
    i = jax.nn.sigmoid(jnp.einsum('blni,nio->blno', xblk, w_i.astype(jnp.float32)).reshape(bsz, l, w)
                       + b_i.astype(jnp.float32))
    log_a = LRU_C * r * jax.nn.log_sigmoid(lam.astype(jnp.float32))
    a = jnp.exp(log_a)
    bx = jnp.sqrt(-jnp.expm1(2.0 * log_a)) * (i * xf)
    return _linear_scan(a, bx).astype(xb.dtype)


def hybrid_layer(h, g_pre, w_in, b_forget, ssm_a_re, ssm_a_im, ssm_log_dt, ssm_b_re, ssm_b_im,
                 ssm_c_re, ssm_c_im, ssm_d, w_glu, b_glu, conv_w, conv_b, w_rgate, b_rgate,
                 w_igate, b_igate, lru_lambda, w_out, g_post):
    bw = BRANCH_WIDTH
    u = rmsnorm(h, g_pre)
    p = u @ w_in
    sizes = [bw] * 4 + [N_FOX_HEADS] + [bw] * 4 + [bw] * 2 + [bw] * 2
    (fq, fk, fv, fgate, f_logit, sq, sk, sv, sgate,
     su, ssm_gate, lx, lru_gate) = jnp.split(p, np.cumsum(sizes)[:-1].tolist(), axis=-1)

    log_f = jax.nn.log_sigmoid(f_logit.astype(jnp.float32) + b_forget.astype(jnp.float32))
    o_fox = forgetting_attention(_split_heads(fq, N_FOX_HEADS), _split_heads(fk, N_FOX_HEADS),
                                 _split_heads(fv, N_FOX_HEADS), log_f.transpose(0, 2, 1))
    o_fox = _merge_heads(o_fox) * jax.nn.silu(fgate)

    o_sb = stick_breaking_attention(_split_heads(sq, N_SB_HEADS), _split_heads(sk, N_SB_HEADS),
                                    _split_heads(sv, N_SB_HEADS))
    o_sb = _merge_heads(o_sb) * jax.nn.silu(sgate)

    o_ssm = s5_mixer(su, ssm_a_re, ssm_a_im, ssm_log_dt, ssm_b_re, ssm_b_im, ssm_c_re, ssm_c_im,
                     ssm_d, w_glu, b_glu) * jax.nn.silu(ssm_gate)

    o_lru = rglru_mixer(lx, conv_w, conv_b, w_rgate, b_rgate, w_igate, b_igate,
                        lru_lambda) * jax.nn.silu(lru_gate)

    mixed = jnp.concatenate([o_fox, o_sb, o_ssm, o_lru], axis=-1) @ w_out
    return h + rmsnorm(mixed, g_post)


def setup_inputs(seed: int = 0) -> dict:
    key = jax.random.key(seed)
    ks = jax.random.split(key, 24)
    f32 = jnp.float32
    bw, g, pst, hg = BRANCH_WIDTH, N_SSM_GROUPS, SSM_STATE, SSM_GROUP
    nrm = lambda k, shape, s: jax.random.normal(k, shape, f32) * s
    x = jax.random.normal(ks[0], (BATCH, SEQ, D_MODEL), f32)
    g_pre = 1.0 + nrm(ks[1], (DEPTH, D_MODEL), 0.02)
    w_in = nrm(ks[2], (DEPTH, D_MODEL, N_IN), D_MODEL ** -0.5)
    b_forget = jax.random.uniform(ks[3], (DEPTH, N_FOX_HEADS), f32, 1.0, 6.0)
    ssm_a_re = -0.5 + nrm(ks[4], (DEPTH, g, pst), 0.01)
    ssm_a_im = math.pi * jnp.arange(pst, dtype=f32) + nrm(ks[5], (DEPTH, g, pst), 0.01)
    ssm_log_dt = jax.random.uniform(ks[6], (DEPTH, g), f32, math.log(1e-3), math.log(1e-1))
    ssm_b_re = nrm(ks[7], (DEPTH, g, pst, hg), (2 * hg) ** -0.5)
    ssm_b_im = nrm(ks[8], (DEPTH, g, pst, hg), (2 * hg) ** -0.5)
    ssm_c_re = nrm(ks[9], (DEPTH, g, hg, pst), (2 * pst) ** -0.5)
    ssm_c_im = nrm(ks[10], (DEPTH, g, hg, pst), (2 * pst) ** -0.5)
    ssm_d = nrm(ks[11], (DEPTH, bw), 1.0)
    w_glu = nrm(ks[12], (DEPTH, bw, bw), bw ** -0.5)
    b_glu = nrm(ks[13], (DEPTH, bw), 0.02)
    conv_w = nrm(ks[14], (DEPTH, CONV_WIDTH, bw), CONV_WIDTH ** -0.5)
    conv_b = nrm(ks[15], (DEPTH, bw), 0.02)
    w_rgate = nrm(ks[16], (DEPTH, LRU_BLOCKS, LRU_BLOCK, LRU_BLOCK), LRU_BLOCK ** -0.5)
    b_rgate = nrm(ks[17], (DEPTH, bw), 0.1)
    w_igate = nrm(ks[18], (DEPTH, LRU_BLOCKS, LRU_BLOCK, LRU_BLOCK), LRU_BLOCK ** -0.5)
    b_igate = nrm(ks[19], (DEPTH, bw), 0.1)
    a_c = jax.random.uniform(ks[20], (DEPTH, bw), f32, 0.9, 0.999)
    s = a_c ** (1.0 / LRU_C)
    lru_lambda = jnp.log(s) - jnp.log1p(-s)
    w_out = nrm(ks[21], (DEPTH, MIX_WIDTH, D_MODEL), MIX_WIDTH ** -0.5)
    g_post = 1.0 + nrm(ks[22], (DEPTH, D_MODEL), 0.02)
    return {"x": x, "g_pre": g_pre, "w_in": w_in, "b_forget": b_forget,
            "ssm_a_re": ssm_a_re, "ssm_a_im": ssm_a_im, "ssm_log_dt": ssm_log_dt,
            "ssm_b_re": ssm_b_re, "ssm_b_im": ssm_b_im, "ssm_c_re": ssm_c_re, "ssm_c_im": ssm_c_im,
            "ssm_d": ssm_d, "w_glu": w_glu, "b_glu": b_glu, "conv_w": conv_w, "conv_b": conv_b,
            "w_rgate": w_rgate, "b_rgate": b_rgate, "w_igate": w_igate, "b_igate": b_igate,
            "lru_lambda": lru_lambda, "w_out": w_out, "g_post": g_post}


def reference(x, g_pre, w_in, b_forget, ssm_a_re, ssm_a_im, ssm_log_dt, ssm_b_re, ssm_b_im,
              ssm_c_re, ssm_c_im, ssm_d, w_glu, b_glu, conv_w, conv_b, w_rgate, b_rgate,
              w_igate, b_igate, lru_lambda, w_out, g_post):
    h = x
    for layer in range(DEPTH):
        h = hybrid_layer(h, g_pre[layer], w_in[layer], b_forget[layer], ssm_a_re[layer],
                         ssm_a_im[layer], ssm_log_dt[layer], ssm_b_re[layer], ssm_b_im[layer],
                         ssm_c_re[layer], ssm_c_im[layer], ssm_d[layer], w_glu[layer], b_glu[layer],
                         conv_w[layer], conv_b[layer], w_rgate[layer], b_rgate[layer],
                         w_igate[layer], b_igate[layer], lru_lambda[layer], w_out[layer],
                         g_post[layer])
    return h
```

```python
import functools
import math

import jax
import jax.numpy as jnp
from jax import lax
from jax.experimental import pallas as pl
from jax.experimental.pallas import tpu as pltpu

F32 = jnp.float32
BF16 = jnp.bfloat16

HEAD_DIM = 128
SSM_GROUP = 16
SSM_STATE = 64
SSM_CHUNK = 32
LRU_BLOCK = 64
LRU_C = 8.0
CONV_WIDTH = 4
RMS_EPS = 1e-6
NEG_BIG = -1e30
SUBLANES = 8
MXU_DIM = 256
VMEM_LIMIT = 56 * 1024 * 1024


def _params(*sem):
    return pltpu.CompilerParams(dimension_semantics=sem, vmem_limit_bytes=VMEM_LIMIT)


def _log_sigmoid(x):
    return jnp.minimum(x, 0.0) - jnp.log1p(jnp.exp(-jnp.abs(x)))


def _silu(x):
    return x * jax.nn.sigmoid(x)


def _rmsnorm_kernel(x_ref, g_ref, u_ref):
    x = x_ref[...]
    ms = jnp.mean(x * x, axis=-1, keepdims=True)
    u_ref[...] = (x * lax.rsqrt(ms + RMS_EPS) * g_ref[...]).astype(u_ref.dtype)


def _rmsnorm(x, g, tm):
    t, d = x.shape
    return pl.pallas_call(
        _rmsnorm_kernel,
        grid=(t // tm,),
        in_specs=[pl.BlockSpec((tm, d), lambda i: (i, 0)),
                  pl.BlockSpec((1, d), lambda i: (0, 0))],
        out_specs=pl.BlockSpec((tm, d), lambda i: (i, 0)),
        out_shape=jax.ShapeDtypeStruct((t, d), BF16),
        compiler_params=_params("parallel"),
    )(x, g.reshape(1, d))


def _post_kernel(y_ref, h_ref, gpost_ref, gnext_ref, hn_ref, u_ref):
    y = y_ref[...]
    ms = jnp.mean(y * y, axis=-1, keepdims=True)
    hn = h_ref[...] + y * lax.rsqrt(ms + RMS_EPS) * gpost_ref[...]
    hn_ref[...] = hn
    ms2 = jnp.mean(hn * hn, axis=-1, keepdims=True)
    u_ref[...] = (hn * lax.rsqrt(ms2 + RMS_EPS) * gnext_ref[...]).astype(u_ref.dtype)


def _post_last_kernel(y_ref, h_ref, gpost_ref, hn_ref):
    y = y_ref[...]
    ms = jnp.mean(y * y, axis=-1, keepdims=True)
    hn_ref[...] = h_ref[...] + y * lax.rsqrt(ms + RMS_EPS) * gpost_ref[...]


def _post(y, h, g_post, g_next, tm):
    t, d = y.shape
    row = pl.BlockSpec((tm, d), lambda i: (i, 0))
    vec = pl.BlockSpec((1, d), lambda i: (0, 0))
    if g_next is None:
        return pl.pallas_call(
            _post_last_kernel, grid=(t // tm,),
            in_specs=[row, row, vec], out_specs=row,
            out_shape=jax.ShapeDtypeStruct((t, d), F32),
            compiler_params=_params("parallel"),
        )(y, h, g_post.reshape(1, d)), None
    return pl.pallas_call(
        _post_kernel, grid=(t // tm,),
        in_specs=[row, row, vec, vec], out_specs=[row, row],
        out_shape=[jax.ShapeDtypeStruct((t, d), F32), jax.ShapeDtypeStruct((t, d), BF16)],
        compiler_params=_params("parallel"),
    )(y, h, g_post.reshape(1, d), g_next.reshape(1, d))


def _inproj_kernel(u_ref, w_ref, wf_ref, bf_ref, p_ref, lf_ref):
    u = u_ref[...]
    p_ref[...] = jnp.dot(u, w_ref[...], preferred_element_type=F32).astype(p_ref.dtype)

    @pl.when(pl.program_id(1) == 0)
    def _():
        logit = jnp.dot(u, wf_ref[...], preferred_element_type=F32) + bf_ref[...]
        lf_ref[...] = _log_sigmoid(logit)


def _inproj(u, w, wf, bf, tm, tn):
    t, d = u.shape
    n = w.shape[1]
    return pl.pallas_call(
        _inproj_kernel,
        grid=(t // tm, n // tn),
        in_specs=[pl.BlockSpec((tm, d), lambda i, j: (i, 0)),
                  pl.BlockSpec((d, tn), lambda i, j: (0, j)),
                  pl.BlockSpec((d, HEAD_DIM), lambda i, j: (0, 0)),
                  pl.BlockSpec((1, HEAD_DIM), lambda i, j: (0, 0))],
        out_specs=[pl.BlockSpec((tm, tn), lambda i, j: (i, j)),
                   pl.BlockSpec((tm, HEAD_DIM), lambda i, j: (i, 0))],
        out_shape=[jax.ShapeDtypeStruct((t, n), BF16),
                   jax.ShapeDtypeStruct((t, HEAD_DIM), F32)],
        compiler_params=_params("parallel", "arbitrary"),
    )(u, w, wf, bf)


def _outproj_kernel(a_ref, b_ref, c_ref, d_ref, w_ref, y_ref):
    x = jnp.concatenate([a_ref[...], b_ref[...], c_ref[...], d_ref[...]], axis=1)
    y_ref[...] = jnp.dot(x, w_ref[...], preferred_element_type=F32)


def _outproj(branches, w, tm, tn):
    t, bw = branches[0].shape
    k, n = w.shape
    bspec = pl.BlockSpec((tm, bw), lambda i, j: (i, 0))
    return pl.pallas_call(
        _outproj_kernel,
        grid=(t // tm, n // tn),
        in_specs=[bspec, bspec, bspec, bspec, pl.BlockSpec((k, tn), lambda i, j: (0, j))],
        out_specs=pl.BlockSpec((tm, tn), lambda i, j: (i, j)),
        out_shape=jax.ShapeDtypeStruct((t, n), F32),
        compiler_params=_params("parallel", "arbitrary"),
    )(*branches, w)


def _cumsum_kernel(lf_ref, tri_ref, o_ref, carry_ref):
    @pl.when(pl.program_id(1) == 0)
    def _():
        carry_ref[...] = jnp.zeros_like(carry_ref)

    x = lf_ref[0]
    hi = x.astype(BF16)
    r1 = x - hi.astype(F32)
    mid = r1.astype(BF16)
    lo = (r1 - mid.astype(F32)).astype(BF16)
    tri = tri_ref[...]
    c = (jnp.dot(tri, hi, preferred_element_type=F32)
         + jnp.dot(tri, mid, preferred_element_type=F32)
         + jnp.dot(tri, lo, preferred_element_type=F32)) + carry_ref[...]
    o_ref[0] = c
    carry_ref[...] = c[-1:, :]


def _forget_cumsum(lf, tc):
    b, l, w = lf.shape
    tri = jnp.tril(jnp.ones((tc, tc), F32)).astype(BF16)
    return pl.pallas_call(
        _cumsum_kernel,
        grid=(b, l // tc),
        in_specs=[pl.BlockSpec((1, tc, w), lambda i, j: (i, j, 0)),
                  pl.BlockSpec((tc, tc), lambda i, j: (0, 0))],
        out_specs=pl.BlockSpec((1, tc, w), lambda i, j: (i, j, 0)),
        out_shape=jax.ShapeDtypeStruct((b, l, w), F32),
        scratch_shapes=[pltpu.VMEM((1, w), F32)],
        compiler_params=_params("parallel", "arbitrary"),
    )(lf, tri)


def _fox_kernel(q_ref, k_ref, v_ref, g_ref, fcol_ref, frow_ref, o_ref, m_sc, l_sc, acc_sc, *, tq, scale):
    h = pl.program_id(1)
    qi = pl.program_id(2)
    q = q_ref[0]
    lane = lax.broadcasted_iota(jnp.int32, (tq, HEAD_DIM), 1)
    fq = jnp.sum(jnp.where(lane == h, fcol_ref[0], 0.0), axis=1, keepdims=True)
    m_sc[...] = jnp.full_like(m_sc, NEG_BIG)
    l_sc[...] = jnp.zeros_like(l_sc)
    acc_sc[...] = jnp.zeros_like(acc_sc)

    def step(ki, diagonal):
        start = pl.multiple_of(ki * tq, tq)
        k = k_ref[0, pl.ds(start, tq), :]
        v = v_ref[0, pl.ds(start, tq), :]
        fk = frow_ref[0, 0, pl.ds(ki, 1), :]
        s = lax.dot_general(q, k, (((1,), (1,)), ((), ())), preferred_element_type=F32) * scale
        s = s + (fq - fk)
        if diagonal:
            row = lax.broadcasted_iota(jnp.int32, (tq, tq), 0)
            col = lax.broadcasted_iota(jnp.int32, (tq, tq), 1)
            s = jnp.where(col <= row, s, NEG_BIG)
        m_prev = m_sc[...]
        m_new = jnp.maximum(m_prev, jnp.max(s, axis=1, keepdims=True))
        alpha = jnp.exp(m_prev - m_new)
        p = jnp.exp(s - m_new)
        l_sc[...] = alpha * l_sc[...] + jnp.sum(p, axis=1, keepdims=True)
        acc_sc[...] = alpha * acc_sc[...] + jnp.dot(p.astype(BF16), v, preferred_element_type=F32)
        m_sc[...] = m_new

    def body(ki, carry):
        step(ki, False)
        return carry

    lax.fori_loop(0, qi, body, 0)
    step(qi, True)
    gate = g_ref[0].astype(F32)
    o_ref[0] = (acc_sc[...] / l_sc[...] * _silu(gate)).astype(o_ref.dtype)


def _fox_attention(p3, cumf, frow, n_heads, tq):
    b, l, _ = p3.shape
    nq = l // tq
    kern = functools.partial(_fox_kernel, tq=tq, scale=HEAD_DIM ** -0.5)
    return pl.pallas_call(
        kern,
        grid=(b, n_heads, nq),
        in_specs=[pl.BlockSpec((1, tq, HEAD_DIM), lambda bi, h, qi: (bi, qi, h)),
                  pl.BlockSpec((1, l, HEAD_DIM), lambda bi, h, qi: (bi, 0, n_heads + h)),
                  pl.BlockSpec((1, l, HEAD_DIM), lambda bi, h, qi: (bi, 0, 2 * n_heads + h)),
                  pl.BlockSpec((1, tq, HEAD_DIM), lambda bi, h, qi: (bi, qi, 3 * n_heads + h)),
                  pl.BlockSpec((1, tq, HEAD_DIM), lambda bi, h, qi: (bi, qi, 0)),
                  pl.BlockSpec((1, 1, nq, tq), lambda bi, h, qi: (bi, h, 0, 0))],
        out_specs=pl.BlockSpec((1, tq, HEAD_DIM), lambda bi, h, qi: (bi, qi, h)),
        out_shape=jax.ShapeDtypeStruct((b, l, n_heads * HEAD_DIM), BF16),
        scratch_shapes=[pltpu.VMEM((tq, 1), F32), pltpu.VMEM((tq, 1), F32),
                        pltpu.VMEM((tq, HEAD_DIM), F32)],
        compiler_params=_params("parallel", "parallel", "arbitrary"),
    )(p3, p3, p3, p3, cumf, frow)


def _sb_kernel(q_ref, k_ref, v_ref, g_ref, tri_ref, o_ref, carry_sc, acc_sc, *, tq, scale):
    qi = pl.program_id(2)
    q = q_ref[0]
    carry_sc[...] = jnp.zeros_like(carry_sc)
    acc_sc[...] = jnp.zeros_like(acc_sc)

    def step(kb, diagonal):
        start = pl.multiple_of(kb * tq, tq)
        k = k_ref[0, pl.ds(start, tq), :]
        v = v_ref[0, pl.ds(start, tq), :]
        z = lax.dot_general(q, k, (((1,), (1,)), ((), ())), preferred_element_type=F32) * scale
        l1p = jnp.log1p(jnp.exp(-jnp.abs(z)))
        ls_pos = jnp.minimum(z, 0.0) - l1p
        log_keep = -jnp.maximum(z, 0.0) - l1p
        if diagonal:
            row = lax.broadcasted_iota(jnp.int32, (tq, tq), 0)
            col = lax.broadcasted_iota(jnp.int32, (tq, tq), 1)
            mask = col < row
            log_keep = jnp.where(mask, log_keep, 0.0)
        within = jnp.dot(log_keep.astype(BF16), tri_ref[...], preferred_element_type=F32)
        w = jnp.exp(ls_pos + within + carry_sc[...])
        if diagonal:
            w = jnp.where(mask, w, 0.0)
        acc_sc[...] += jnp.dot(w.astype(BF16), v, preferred_element_type=F32)
        carry_sc[...] += jnp.sum(log_keep, axis=1, keepdims=True)

    step(qi, True)

    def body(s, carry):
        step(qi - 1 - s, False)
        return carry

    lax.fori_loop(0, qi, body, 0)
    gate = g_ref[0].astype(F32)
    o_ref[0] = (acc_sc[...] * _silu(gate)).astype(o_ref.dtype)


def _sb_attention(p3, col0, n_heads, tq):
    b, l, _ = p3.shape
    nq = l // tq
    tri = jnp.tril(jnp.ones((tq, tq), F32), k=-1).astype(BF16)
    kern = functools.partial(_sb_kernel, tq=tq, scale=HEAD_DIM ** -0.5)
    return pl.pallas_call(
        kern,
        grid=(b, n_heads, nq),
        in_specs=[pl.BlockSpec((1, tq, HEAD_DIM), lambda bi, h, qi: (bi, qi, col0 + h)),
                  pl.BlockSpec((1, l, HEAD_DIM), lambda bi, h, qi: (bi, 0, col0 + n_heads + h)),
                  pl.BlockSpec((1, l, HEAD_DIM), lambda bi, h, qi: (bi, 0, col0 + 2 * n_heads + h)),
                  pl.BlockSpec((1, tq, HEAD_DIM), lambda bi, h, qi: (bi, qi, col0 + 3 * n_heads + h)),
                  pl.BlockSpec((tq, tq), lambda bi, h, qi: (0, 0))],
        out_specs=pl.BlockSpec((1, tq, HEAD_DIM), lambda bi, h, qi: (bi, qi, h)),
        out_shape=jax.ShapeDtypeStruct((b, l, n_heads * HEAD_DIM), BF16),
        scratch_shapes=[pltpu.VMEM((tq, 1), F32), pltpu.VMEM((tq, HEAD_DIM), F32)],
        compiler_params=_params("parallel", "parallel", "arbitrary"),
    )(p3, p3, p3, p3, tri)


def _ssm_weights(a_re, a_im, log_dt, b_re, b_im, c_re, c_im):
    g, p = a_re.shape
    hg = b_re.shape[-1]
    c = SSM_CHUNK
    dt = jnp.exp(log_dt.astype(F32))[:, None]
    a_re = a_re.astype(F32)
    a_im = a_im.astype(F32)
    n = jnp.arange(c + 1, dtype=F32)[:, None, None]
    mag = jnp.exp(a_re * dt * n)
    ang = a_im * dt * n
    pw_re, pw_im = mag * jnp.cos(ang), mag * jnp.sin(ang)
    num_re, num_im = pw_re[1] - 1.0, pw_im[1]
    den = a_re * a_re + a_im * a_im
    f_re = (num_re * a_re + num_im * a_im) / den
    f_im = (num_im * a_re - num_re * a_im) / den
    bb_re = f_re[..., None] * b_re - f_im[..., None] * b_im
    bb_im = f_re[..., None] * b_im + f_im[..., None] * b_re
    cc_re, cc_im = c_re.astype(F32), c_im.astype(F32)
    hi = lax.Precision.HIGHEST
    cl_re = cc_re[None] * pw_re[:, :, None, :] - cc_im[None] * pw_im[:, :, None, :]
    cl_im = cc_re[None] * pw_im[:, :, None, :] + cc_im[None] * pw_re[:, :, None, :]
    kk = (jnp.einsum('ngop,gpi->ngoi', cl_re[:c], bb_re, precision=hi)
          - jnp.einsum('ngop,gpi->ngoi', cl_im[:c], bb_im, precision=hi))
    s_idx = jnp.arange(c)[:, None]
    t_idx = jnp.arange(c)[None, :]
    lag = t_idx - s_idx
    kt = jnp.where((lag >= 0)[:, :, None, None, None], kk[jnp.clip(lag, 0, c - 1)], 0.0)
    m = kt.transpose(2, 0, 4, 1, 3).reshape(g, c * hg, c * hg)
    pr, pi_ = pw_re[:c][::-1], pw_im[:c][::-1]
    be_re = (pr[..., None] * bb_re[None] - pi_[..., None] * bb_im[None])
    be_im = (pr[..., None] * bb_im[None] + pi_[..., None] * bb_re[None])
    be_re = be_re.transpose(1, 0, 3, 2).reshape(g, c * hg, p)
    be_im = be_im.transpose(1, 0, 3, 2).reshape(g, c * hg, p)
    co_re = cl_re[1:].transpose(1, 3, 0, 2).reshape(g, p, c * hg)
    co_im = (-cl_im[1:]).transpose(1, 3, 0, 2).reshape(g, p, c * hg)

    def pair_diag(x):
        x = x.reshape(g // 2, 2, *x.shape[1:])
        z = jnp.zeros_like(x[:, 0])
        top = jnp.concatenate([x[:, 0], z], axis=2)
        bot = jnp.concatenate([z, x[:, 1]], axis=2)
        return jnp.concatenate([top, bot], axis=1)

    lam_c = jnp.stack([pw_re[c], pw_im[c]], axis=0).reshape(2, g // 2, 1, 2 * p)
    return (pair_diag(m).astype(BF16), pair_diag(be_re).astype(BF16), pair_diag(be_im).astype(BF16),
            pair_diag(co_re).astype(BF16), pair_diag(co_im).astype(BF16), lam_c[0], lam_c[1])


def _ssm_kernel(u_ref, m_ref, bre_ref, bim_ref, cre_ref, cim_ref, lre_ref, lim_ref, y_ref,
                zre_sc, zim_sc, *, n_batch, n_chunk):
    u = u_ref[0]
    zre_sc[...] = jnp.dot(u, bre_ref[0], preferred_element_type=F32)
    zim_sc[...] = jnp.dot(u, bim_ref[0], preferred_element_type=F32)
    lre = lre_ref[0]
    lim = lim_ref[0]

    def body(kc, carry):
        new = []
        for bi in range(n_batch):
            xr, xi = carry[bi]
            row = bi * n_chunk + kc
            zr = zre_sc[pl.ds(row, 1), :]
            zi = zim_sc[pl.ds(row, 1), :]
            zre_sc[pl.ds(row, 1), :] = xr
            zim_sc[pl.ds(row, 1), :] = xi
            new.append((lre * xr - lim * xi + zr, lre * xi + lim * xr + zi))
        return tuple(new)

    zero = jnp.zeros((1, zre_sc.shape[1]), F32)
    lax.fori_loop(0, n_chunk, body, tuple((zero, zero) for _ in range(n_batch)))
    y = (jnp.dot(u, m_ref[0], preferred_element_type=F32)
         + jnp.dot(zre_sc[...].astype(BF16), cre_ref[0], preferred_element_type=F32)
         + jnp.dot(zim_sc[...].astype(BF16), cim_ref[0], preferred_element_type=F32))
    y_ref[0] = y.astype(y_ref.dtype)


def _ssm_scan(su, weights):
    b, l, w = su.shape
    m, bre, bim, cre, cim, lre, lim = weights
    npair = m.shape[0]
    c, hg = SSM_CHUNK, SSM_GROUP
    nck = l // c
    rows = b * nck
    lanes = 2 * c * hg
    sp = 2 * SSM_STATE
    u = su.reshape(b, nck, c, npair, 2, hg).transpose(3, 0, 1, 4, 2, 5).reshape(npair, rows, lanes)
    kern = functools.partial(_ssm_kernel, n_batch=b, n_chunk=nck)
    per_pair = lambda r, cc: pl.BlockSpec((1, r, cc), lambda i: (i, 0, 0))
    y = pl.pallas_call(
        kern,
        grid=(npair,),
        in_specs=[per_pair(rows, lanes), per_pair(lanes, lanes), per_pair(lanes, sp), per_pair(lanes, sp),
                  per_pair(sp, lanes), per_pair(sp, lanes), per_pair(1, sp), per_pair(1, sp)],
        out_specs=per_pair(rows, lanes),
        out_shape=jax.ShapeDtypeStruct((npair, rows, lanes), BF16),
        scratch_shapes=[pltpu.VMEM((rows, sp), F32), pltpu.VMEM((rows, sp), F32)],
        compiler_params=_params("parallel"),
    )(u, m, bre, bim, cre, cim, lre, lim)
    return y.reshape(npair, b, nck, 2, c, hg).transpose(1, 2, 4, 0, 3, 5).reshape(b, l, w)


def _glu_kernel(y_ref, u_ref, g_ref, d_ref, w_ref, bg_ref, o_ref):
    y = y_ref[...].astype(F32) + d_ref[...] * u_ref[...].astype(F32)
    y = jax.nn.gelu(y)
    z = jnp.dot(y.astype(BF16), w_ref[...], preferred_element_type=F32) + bg_ref[...]
    o_ref[...] = (y * jax.nn.sigmoid(z) * _silu(g_ref[...].astype(F32))).astype(o_ref.dtype)


def _ssm_glu(y, p2, u_blk, gate_blk, d_skip, w_glu, b_glu, tm):
    t, bw = y.shape
    return pl.pallas_call(
        _glu_kernel,
        grid=(t // tm,),
        in_specs=[pl.BlockSpec((tm, bw), lambda i: (i, 0)),
                  pl.BlockSpec((tm, bw), lambda i: (i, u_blk)),
                  pl.BlockSpec((tm, bw), lambda i: (i, gate_blk)),
                  pl.BlockSpec((1, bw), lambda i: (0, 0)),
                  pl.BlockSpec((bw, bw), lambda i: (0, 0)),
                  pl.BlockSpec((1, bw), lambda i: (0, 0))],
        out_specs=pl.BlockSpec((tm, bw), lambda i: (i, 0)),
        out_shape=jax.ShapeDtypeStruct((t, bw), BF16),
        compiler_params=_params("parallel"),
    )(y, p2, p2, d_skip.reshape(1, bw).astype(F32), w_glu.astype(BF16), b_glu.reshape(1, bw).astype(F32))


def _lru_kernel(x_ref, g_ref, cw_ref, cb_ref, wg_ref, br_ref, bi_ref, lam_ref, o_ref,
                tail_sc, h_sc, a_sc, b_sc, *, tl):
    @pl.when(pl.program_id(1) == 0)
    def _():
        tail_sc[...] = jnp.zeros_like(tail_sc)
        h_sc[...] = jnp.zeros_like(h_sc)

    x = x_ref[0].astype(F32)
    w = x.shape[1]
    xcat = jnp.concatenate([tail_sc[...], x], axis=0)
    xc = cb_ref[...] + cw_ref[CONV_WIDTH - 1:CONV_WIDTH, :] * x
    for d in range(1, CONV_WIDTH):
        xd = pltpu.roll(xcat, d, 0)[SUBLANES:, :]
        xc = xc + cw_ref[CONV_WIDTH - 1 - d:CONV_WIDTH - d, :] * xd
    tail_sc[...] = x[tl - SUBLANES:, :]

    xb = xc.astype(BF16)
    nblk = w // MXU_DIM
    gates = [jnp.dot(xb[:, kb * MXU_DIM:(kb + 1) * MXU_DIM], wg_ref[kb], preferred_element_type=F32)
             for kb in range(nblk)]
    gr = jnp.concatenate([gt[:, :MXU_DIM] for gt in gates], axis=1)
    gi = jnp.concatenate([gt[:, MXU_DIM:] for gt in gates], axis=1)
    r = jax.nn.sigmoid(gr + br_ref[...])
    i = jax.nn.sigmoid(gi + bi_ref[...])
    log_a = r * (LRU_C * _log_sigmoid(lam_ref[...]))
    a = jnp.exp(log_a)
    bx = jnp.sqrt(1.0 - a * a) * (i * xc)

    sub = lax.broadcasted_iota(jnp.int32, (tl, w), 0) % SUBLANES
    d = 1
    while d < SUBLANES:
        a_prev = pltpu.roll(a, d, 0)
        b_prev = pltpu.roll(bx, d, 0)
        use = sub >= d
        bx = jnp.where(use, a * b_prev + bx, bx)
        a = jnp.where(use, a * a_prev, a)
        d *= 2
    a_sc[...] = a
    b_sc[...] = bx

    def body(kt, h_last):
        s = pl.multiple_of(kt * SUBLANES, SUBLANES)
        ht = a_sc[pl.ds(s, SUBLANES), :] * h_last + b_sc[pl.ds(s, SUBLANES), :]
        b_sc[pl.ds(s, SUBLANES), :] = ht
        return ht[SUBLANES - 1:, :]

    h_sc[...] = lax.fori_loop(0, tl // SUBLANES, body, h_sc[...])
    o_ref[0] = (b_sc[...] * _silu(g_ref[0].astype(F32))).astype(o_ref.dtype)


def _lru(p3, x_blk, gate_blk, conv_w, conv_b, w_r, b_r, w_i, b_i, lam, tl):
    b, l, _ = p3.shape
    bw = conv_w.shape[1]
    per = MXU_DIM // LRU_BLOCK
    nblk = bw // MXU_DIM

    def tile_diag(wg):
        wg = wg.astype(F32).reshape(nblk, per, LRU_BLOCK, LRU_BLOCK)
        eye = jnp.eye(per, dtype=F32)
        return jnp.einsum('kaio,ab->kaibo', wg, eye).reshape(nblk, MXU_DIM, MXU_DIM)

    wg = jnp.concatenate([tile_diag(w_r), tile_diag(w_i)], axis=2).astype(BF16)
    vec = lambda a: a.reshape(1, bw).astype(F32)
    kern = functools.partial(_lru_kernel, tl=tl)
    full = lambda shape: pl.BlockSpec(shape, lambda bi, ti: (0,) * len(shape))
    return pl.pallas_call(
        kern,
        grid=(b, l // tl),
        in_specs=[pl.BlockSpec((1, tl, bw), lambda bi, ti: (bi, ti, x_blk)),
                  pl.BlockSpec((1, tl, bw), lambda bi, ti: (bi, ti, gate_blk)),
                  full((CONV_WIDTH, bw)), full((1, bw)), full((nblk, MXU_DIM, 2 * MXU_DIM)),
                  full((1, bw)), full((1, bw)), full((1, bw))],
        out_specs=pl.BlockSpec((1, tl, bw), lambda bi, ti: (bi, ti, 0)),
        out_shape=jax.ShapeDtypeStruct((b, l, bw), BF16),
        scratch_shapes=[pltpu.VMEM((SUBLANES, bw), F32), pltpu.VMEM((1, bw), F32),
                        pltpu.VMEM((tl, bw), F32), pltpu.VMEM((tl, bw), F32)],
        compiler_params=_params("parallel", "arbitrary"),
    )(p3, p3, conv_w.astype(F32), vec(conv_b), wg, vec(b_r), vec(b_i), vec(lam))


def _tiles(b, l):
    t = b * l
    return dict(
        norm=min(256, t),
        mm_m=min(1024, t),
        mm_n=1024,
        cum=min(512, l),
        fox=min(512, l),
        sb=min(MXU_DIM, l),
        glu=min(512, t),
        lru=min(512, l),
    )


def kernel(x, g_pre, w_in, b_forget, ssm_a_re, ssm_a_im, ssm_log_dt, ssm_b_re, ssm_b_im, ssm_c_re, ssm_c_im, ssm_d, w_glu, b_glu, conv_w, conv_b, w_rgate, b_rgate, w_igate, b_igate, lru_lambda, w_out, g_post):
    b, l, d = x.shape
    depth = w_in.shape[0]
    bw = d // 4
    nh = bw // HEAD_DIM
    t = b * l
    ts = _tiles(b, l)
    assert w_in.shape[2] == 12 * bw + nh and l % SSM_CHUNK == 0

    h = x.reshape(t, d)
    u = _rmsnorm(h, g_pre[0], ts["norm"])
    for layer in range(depth):
        wl = w_in[layer]
        w_main = jnp.concatenate([wl[:, :4 * bw], wl[:, 4 * bw + nh:]], axis=1).astype(BF16)
        w_f = jnp.pad(wl[:, 4 * bw:4 * bw + nh], ((0, 0), (0, HEAD_DIM - nh))).astype(BF16)
        b_f = jnp.pad(b_forget[layer].astype(F32), (0, HEAD_DIM - nh)).reshape(1, HEAD_DIM)
        p2, log_f = _inproj(u, w_main, w_f, b_f, ts["mm_m"], ts["mm_n"])
        p3 = p2.reshape(b, l, 12 * bw)

        cumf = _forget_cumsum(log_f.reshape(b, l, HEAD_DIM), ts["cum"])
        frow = cumf[:, :, :nh].transpose(0, 2, 1).reshape(b, nh, l // ts["fox"], ts["fox"])
        o_fox = _fox_attention(p3, cumf, frow, nh, ts["fox"])
        o_sb = _sb_attention(p3, 4 * nh, nh, ts["sb"])

        ssm_w = _ssm_weights(ssm_a_re[layer], ssm_a_im[layer], ssm_log_dt[layer], ssm_b_re[layer],
                             ssm_b_im[layer], ssm_c_re[layer], ssm_c_im[layer])
        y_ssm = _ssm_scan(p3[:, :, 8 * bw:9 * bw], ssm_w)
        o_ssm = _ssm_glu(y_ssm.reshape(t, bw), p2, 8, 9, ssm_d[layer], w_glu[layer], b_glu[layer], ts["glu"])

        o_lru = _lru(p3, 10, 11, conv_w[layer], conv_b[layer], w_rgate[layer], b_rgate[layer],
                     w_igate[layer], b_igate[layer], lru_lambda[layer], ts["lru"])

        y = _outproj([o_fox.reshape(t, bw), o_sb.reshape(t, bw), o_ssm, o_lru.reshape(t, bw)],
                     w_out[layer].astype(BF16), ts["mm_m"], ts["mm_n"])
        g_next = g_pre[layer + 1] if layer + 1 < depth else None
        h, u = _post(y, h, g_post[layer], g_next, ts["norm"])
    return h.reshape(b, l, d)
```

```python
import functools
import math

import jax
import jax.numpy as jnp
from jax import lax
from jax.experimental import pallas as pl
from jax.experimental.pallas import tpu as pltpu

F32 = jnp.float32
BF16 = jnp.bfloat16

HEAD_DIM = 128
SSM_GROUP = 16
SSM_STATE = 64
SSM_UNIT = 128
SSM_CHUNK = 8
LRU_BLOCK = 64
LRU_C = 8.0
CONV_WIDTH = 4
RMS_EPS = 1e-6
NEG_BIG = -1e30
SB_EXP_UNDERFLOW = 110.0
ROW_GROUP = 256
SUBLANES = 8
MXU_DIM = 256
VMEM_LIMIT = 56 * 1024 * 1024


def _params(*sem):
    return pltpu.CompilerParams(dimension_semantics=sem, vmem_limit_bytes=VMEM_LIMIT)


def _log_sigmoid(x):
    return jnp.minimum(x, 0.0) - jnp.log1p(jnp.exp(-jnp.abs(x)))


def _silu(x):
    return x * jax.nn.sigmoid(x)


def _rmsnorm_kernel(x_ref, g_ref, u_ref):
    x = x_ref[...]
    ms = jnp.mean(x * x, axis=-1, keepdims=True)
    u_ref[...] = (x * lax.rsqrt(ms + RMS_EPS) * g_ref[...]).astype(u_ref.dtype)


def _rmsnorm(x, g, tm):
    t, d = x.shape
    return pl.pallas_call(
        _rmsnorm_kernel,
        grid=(t // tm,),
        in_specs=[pl.BlockSpec((tm, d), lambda i: (i, 0)),
                  pl.BlockSpec((1, d), lambda i: (0, 0))],
        out_specs=pl.BlockSpec((tm, d), lambda i: (i, 0)),
        out_shape=jax.ShapeDtypeStruct((t, d), BF16),
        compiler_params=_params("parallel"), name="rmsnorm_first",
    )(x, g.reshape(1, d))


def _post_kernel(y_ref, h_ref, gpost_ref, gnext_ref, hn_ref, u_ref):
    y = y_ref[...]
    ms = jnp.mean(y * y, axis=-1, keepdims=True)
    hn = h_ref[...] + y * lax.rsqrt(ms + RMS_EPS) * gpost_ref[...]
    hn_ref[...] = hn
    ms2 = jnp.mean(hn * hn, axis=-1, keepdims=True)
    u_ref[...] = (hn * lax.rsqrt(ms2 + RMS_EPS) * gnext_ref[...]).astype(u_ref.dtype)


def _post_last_kernel(y_ref, h_ref, gpost_ref, hn_ref):
    y = y_ref[...]
    ms = jnp.mean(y * y, axis=-1, keepdims=True)
    hn_ref[...] = h_ref[...] + y * lax.rsqrt(ms + RMS_EPS) * gpost_ref[...]


def _post(y, h, g_post, g_next, tm):
    t, d = y.shape
    row = pl.BlockSpec((tm, d), lambda i: (i, 0))
    vec = pl.BlockSpec((1, d), lambda i: (0, 0))
    if g_next is None:
        return pl.pallas_call(
            _post_last_kernel, grid=(t // tm,),
            in_specs=[row, row, vec], out_specs=row,
            out_shape=jax.ShapeDtypeStruct((t, d), F32),
            compiler_params=_params("parallel"), name="post_norm_last",
        )(y, h, g_post.reshape(1, d)), None
    return pl.pallas_call(
        _post_kernel, grid=(t // tm,),
        in_specs=[row, row, vec, vec], out_specs=[row, row],
        out_shape=[jax.ShapeDtypeStruct((t, d), F32), jax.ShapeDtypeStruct((t, d), BF16)],
        compiler_params=_params("parallel"), name="post_norm",
    )(y, h, g_post.reshape(1, d), g_next.reshape(1, d))


def _inproj_kernel(u_ref, w_ref, wf_ref, bf_ref, p_ref, lf_ref):
    u = u_ref[...]
    p_ref[...] = jnp.dot(u, w_ref[...], preferred_element_type=F32).astype(p_ref.dtype)

    @pl.when(pl.program_id(1) == 0)
    def _():
        logit = jnp.dot(u, wf_ref[...], preferred_element_type=F32) + bf_ref[...]
        lf_ref[...] = _log_sigmoid(logit)


def _inproj(u, w, wf, bf, tm, tn):
    t, d = u.shape
    n = w.shape[1]
    return pl.pallas_call(
        _inproj_kernel,
        grid=(t // tm, n // tn),
        in_specs=[pl.BlockSpec((tm, d), lambda i, j: (i, 0)),
                  pl.BlockSpec((d, tn), lambda i, j: (0, j)),
                  pl.BlockSpec((d, HEAD_DIM), lambda i, j: (0, 0)),
                  pl.BlockSpec((1, HEAD_DIM), lambda i, j: (0, 0))],
        out_specs=[pl.BlockSpec((tm, tn), lambda i, j: (i, j)),
                   pl.BlockSpec((tm, HEAD_DIM), lambda i, j: (i, 0))],
        out_shape=[jax.ShapeDtypeStruct((t, n), BF16),
                   jax.ShapeDtypeStruct((t, HEAD_DIM), F32)],
        compiler_params=_params("parallel", "arbitrary"), name="in_proj",
    )(u, w, wf, bf)


def _outproj_kernel(a_ref, b_ref, c_ref, d_ref, w_ref, y_ref):
    x = jnp.concatenate([a_ref[...], b_ref[...], c_ref[...], d_ref[...]], axis=1)
    y_ref[...] = jnp.dot(x, w_ref[...], preferred_element_type=F32)


def _outproj(branches, w, tm, tn):
    t, bw = branches[0].shape
    k, n = w.shape
    bspec = pl.BlockSpec((tm, bw), lambda i, j: (i, 0))
    return pl.pallas_call(
        _outproj_kernel,
        grid=(t // tm, n // tn),
        in_specs=[bspec, bspec, bspec, bspec, pl.BlockSpec((k, tn), lambda i, j: (0, j))],
        out_specs=pl.BlockSpec((tm, tn), lambda i, j: (i, j)),
        out_shape=jax.ShapeDtypeStruct((t, n), F32),
        compiler_params=_params("parallel", "arbitrary"), name="out_proj",
    )(*branches, w)


def _cumsum_kernel(lf_ref, tri_ref, o_ref, carry_ref):
    @pl.when(pl.program_id(1) == 0)
    def _():
        carry_ref[...] = jnp.zeros_like(carry_ref)

    x = lf_ref[0]
    hi = x.astype(BF16)
    r1 = x - hi.astype(F32)
    mid = r1.astype(BF16)
    lo = (r1 - mid.astype(F32)).astype(BF16)
    tri = tri_ref[...]
    c = (jnp.dot(tri, hi, preferred_element_type=F32)
         + jnp.dot(tri, mid, preferred_element_type=F32)
         + jnp.dot(tri, lo, preferred_element_type=F32)) + carry_ref[...]
    o_ref[0] = c
    carry_ref[...] = c[-1:, :]


def _forget_cumsum(lf, tc):
    b, l, w = lf.shape
    tri = jnp.tril(jnp.ones((tc, tc), F32)).astype(BF16)
    return pl.pallas_call(
        _cumsum_kernel,
        grid=(b, l // tc),
        in_specs=[pl.BlockSpec((1, tc, w), lambda i, j: (i, j, 0)),
                  pl.BlockSpec((tc, tc), lambda i, j: (0, 0))],
        out_specs=pl.BlockSpec((1, tc, w), lambda i, j: (i, j, 0)),
        out_shape=jax.ShapeDtypeStruct((b, l, w), F32),
        scratch_shapes=[pltpu.VMEM((1, w), F32)],
        compiler_params=_params("parallel", "arbitrary"), name="forget_cumsum",
    )(lf, tri)


def _fox_kernel(q_ref, k_ref, v_ref, g_ref, fcol_ref, frow_ref, o_ref, fq_sc, m_sc, l_sc, acc_sc, *, tq, rs):
    h = pl.program_id(1)
    qi = pl.program_id(2)
    lane = lax.broadcasted_iota(jnp.int32, (tq, HEAD_DIM), 1)
    fq = jnp.sum(jnp.where(lane == h, fcol_ref[0], 0.0), axis=1, keepdims=True)
    fq_sc[...] = jnp.broadcast_to(fq, fq_sc.shape)
    m_sc[...] = jnp.full_like(m_sc, NEG_BIG)
    l_sc[...] = jnp.zeros_like(l_sc)
    acc_sc[...] = jnp.zeros_like(acc_sc)

    nrep = tq // HEAD_DIM

    def step(ki, diagonal):
        start = pl.multiple_of(ki * tq, tq)
        k = k_ref[0, pl.ds(start, tq), :]
        v = v_ref[0, pl.ds(start, tq), :]
        fk = frow_ref[0, 0, pl.ds(ki, 1), :]
        for r0 in range(0, tq, rs):
            rows = pl.ds(r0, rs)
            s = lax.dot_general(q_ref[0, rows, :], k, (((1,), (1,)), ((), ())), preferred_element_type=F32)
            s = s + (jnp.tile(fq_sc[rows, :], (1, nrep)) - fk)
            if diagonal:
                row = lax.broadcasted_iota(jnp.int32, (rs, tq), 0) + r0
                col = lax.broadcasted_iota(jnp.int32, (rs, tq), 1)
                s = jnp.where(col <= row, s, NEG_BIG)
            m_prev = m_sc[rows, :]
            m_new = jnp.maximum(m_prev, jnp.max(s, axis=1, keepdims=True))
            alpha = jnp.exp(m_prev - m_new)
            p = jnp.exp(s - jnp.tile(m_new, (1, nrep)))
            l_sc[rows, :] = alpha * l_sc[rows, :] + jnp.sum(p, axis=1, keepdims=True)
            acc_sc[rows, :] = alpha * acc_sc[rows, :] + jnp.dot(p.astype(BF16), v, preferred_element_type=F32)
            m_sc[rows, :] = m_new

    def body(ki, carry):
        step(ki, False)
        return carry

    lax.fori_loop(0, qi, body, 0)
    step(qi, True)
    gate = g_ref[0].astype(F32)
    o_ref[0] = (acc_sc[...] / l_sc[...] * _silu(gate)).astype(o_ref.dtype)


def _fox_attention(p3, cumf, frow, n_heads, tq):
    b, l, _ = p3.shape
    nq = l // tq
    kern = functools.partial(_fox_kernel, tq=tq, rs=min(ROW_GROUP, tq))
    return pl.pallas_call(
        kern,
        grid=(b, n_heads, nq),
        in_specs=[pl.BlockSpec((1, tq, HEAD_DIM), lambda bi, h, qi: (bi, qi, h)),
                  pl.BlockSpec((1, l, HEAD_DIM), lambda bi, h, qi: (bi, 0, n_heads + h)),
                  pl.BlockSpec((1, l, HEAD_DIM), lambda bi, h, qi: (bi, 0, 2 * n_heads + h)),
                  pl.BlockSpec((1, tq, HEAD_DIM), lambda bi, h, qi: (bi, qi, 3 * n_heads + h)),
                  pl.BlockSpec((1, tq, HEAD_DIM), lambda bi, h, qi: (bi, qi, 0)),
                  pl.BlockSpec((1, 1, nq, tq), lambda bi, h, qi: (bi, h, 0, 0))],
        out_specs=pl.BlockSpec((1, tq, HEAD_DIM), lambda bi, h, qi: (bi, qi, h)),
        out_shape=jax.ShapeDtypeStruct((b, l, n_heads * HEAD_DIM), BF16),
        scratch_shapes=[pltpu.VMEM((tq, HEAD_DIM), F32)] * 4,
        compiler_params=_params("parallel", "parallel", "arbitrary"), name="fox_attn",
    )(p3, p3, p3, p3, cumf, frow)


def _sb_kernel(q_ref, k_ref, v_ref, g_ref, tri_ref, o_ref, cost_sc, acc_sc, *, tq, tk, rs):
    qi = pl.program_id(2)
    nd = tq // tk
    cost_sc[...] = jnp.zeros_like(cost_sc)
    acc_sc[...] = jnp.zeros_like(acc_sc)

    def step(kb, diagonal):
        start = pl.multiple_of(kb * tk, tk)
        k = k_ref[0, pl.ds(start, tk), :]
        v = v_ref[0, pl.ds(start, tk), :]
        for r0 in range(0, tq, rs):
            rows = pl.ds(r0, rs)
            z = lax.dot_general(q_ref[0, rows, :], k, (((1,), (1,)), ((), ())), preferred_element_type=F32)
            softplus = jnp.maximum(z, 0.0) + jnp.log(1.0 + jnp.exp(-jnp.abs(z)))
            log_beta = z - softplus
            if diagonal:
                row = lax.broadcasted_iota(jnp.int32, (rs, tk), 0) + (qi * tq + r0)
                col = lax.broadcasted_iota(jnp.int32, (rs, tk), 1) + kb * tk
                mask = col < row
                softplus = jnp.where(mask, softplus, 0.0)
            within = jnp.dot(softplus.astype(BF16), tri_ref[...], preferred_element_type=F32)
            w = jnp.exp(log_beta - within - jnp.tile(cost_sc[rows, :], (1, tk // HEAD_DIM)))
            if diagonal:
                w = jnp.where(mask, w, 0.0)
            acc_sc[rows, :] += jnp.dot(w.astype(BF16), v, preferred_element_type=F32)
            cost_sc[rows, :] += jnp.sum(softplus, axis=1, keepdims=True)

    for j in reversed(range(nd)):
        step(qi * nd + j, True)

    def cond(state):
        return jnp.logical_and(state[0] >= 0, state[1] < SB_EXP_UNDERFLOW)

    def body(state):
        step(state[0], False)
        return state[0] - 1, jnp.min(cost_sc[...])

    lax.while_loop(cond, body, (qi * nd - 1, jnp.min(cost_sc[...])))
    gate = g_ref[0].astype(F32)
    o_ref[0] = (acc_sc[...] * _silu(gate)).astype(o_ref.dtype)


def _sb_attention(p3, col0, n_heads, tq, tk):
    b, l, _ = p3.shape
    nq = l // tq
    tri = jnp.tril(jnp.ones((tk, tk), F32), k=-1).astype(BF16)
    kern = functools.partial(_sb_kernel, tq=tq, tk=tk, rs=min(ROW_GROUP, tq))
    return pl.pallas_call(
        kern,
        grid=(b, n_heads, nq),
        in_specs=[pl.BlockSpec((1, tq, HEAD_DIM), lambda bi, h, qi: (bi, qi, col0 + h)),
                  pl.BlockSpec((1, l, HEAD_DIM), lambda bi, h, qi: (bi, 0, col0 + n_heads + h)),
                  pl.BlockSpec((1, l, HEAD_DIM), lambda bi, h, qi: (bi, 0, col0 + 2 * n_heads + h)),
                  pl.BlockSpec((1, tq, HEAD_DIM), lambda bi, h, qi: (bi, qi, col0 + 3 * n_heads + h)),
                  pl.BlockSpec((tk, tk), lambda bi, h, qi: (0, 0))],
        out_specs=pl.BlockSpec((1, tq, HEAD_DIM), lambda bi, h, qi: (bi, qi, h)),
        out_shape=jax.ShapeDtypeStruct((b, l, n_heads * HEAD_DIM), BF16),
        scratch_shapes=[pltpu.VMEM((tq, HEAD_DIM), F32)] * 2,
        compiler_params=_params("parallel", "parallel", "arbitrary"), name="sb_attn",
    )(p3, p3, p3, p3, tri)


def _ssm_weights(a_re, a_im, log_dt, b_re, b_im, c_re, c_im):
    g, p = a_re.shape
    hg = b_re.shape[-1]
    c = SSM_CHUNK
    dt = jnp.exp(log_dt.astype(F32))[:, None]
    a_re = a_re.astype(F32)
    a_im = a_im.astype(F32)
    n = jnp.arange(c + 1, dtype=F32)[:, None, None]
    mag = jnp.exp(a_re * dt * n)
    ang = a_im * dt * n
    pw_re, pw_im = mag * jnp.cos(ang), mag * jnp.sin(ang)
    num_re, num_im = pw_re[1] - 1.0, pw_im[1]
    den = a_re * a_re + a_im * a_im
    f_re = (num_re * a_re + num_im * a_im) / den
    f_im = (num_im * a_re - num_re * a_im) / den
    bb_re = f_re[..., None] * b_re - f_im[..., None] * b_im
    bb_im = f_re[..., None] * b_im + f_im[..., None] * b_re
    cc_re, cc_im = c_re.astype(F32), c_im.astype(F32)
    hi = lax.Precision.HIGHEST
    cl_re = cc_re[None] * pw_re[:, :, None, :] - cc_im[None] * pw_im[:, :, None, :]
    cl_im = cc_re[None] * pw_im[:, :, None, :] + cc_im[None] * pw_re[:, :, None, :]
    kk = (jnp.einsum('ngop,gpi->ngoi', cl_re[:c], bb_re, precision=hi)
          - jnp.einsum('ngop,gpi->ngoi', cl_im[:c], bb_im, precision=hi))
    ug = SSM_UNIT // hg
    nu = g // ug
    eye = jnp.eye(ug, dtype=F32)
    lag = jnp.arange(c)[None, :] - jnp.arange(c)[:, None]
    kt = jnp.where((lag >= 0)[:, :, None, None, None], kk[jnp.clip(lag, 0, c - 1)], 0.0)
    m = jnp.einsum('stugoi,gh->usgitho', kt.reshape(c, c, nu, ug, hg, hg), eye)
    m = m.reshape(nu, c * SSM_UNIT, c * SSM_UNIT)
    n_rev = jnp.arange(c - 1, -1, -1, dtype=F32)[:, None, None]
    pr = jnp.exp(a_re * dt * n_rev) * jnp.cos(a_im * dt * n_rev)
    pi_ = jnp.exp(a_re * dt * n_rev) * jnp.sin(a_im * dt * n_rev)
    be_re = pr[..., None] * bb_re[None] - pi_[..., None] * bb_im[None]
    be_im = pr[..., None] * bb_im[None] + pi_[..., None] * bb_re[None]

    def bend(x):
        x = jnp.einsum('sugpi,gh->usgihp', x.reshape(c, nu, ug, p, hg), eye)
        return x.reshape(nu, c * SSM_UNIT, ug * p).astype(BF16)

    def cout(x):
        x = jnp.einsum('tugop,gh->ugptho', x.reshape(c, nu, ug, hg, p), eye)
        return x.reshape(nu, ug * p, c * SSM_UNIT).astype(BF16)

    lam_re = pw_re[c].reshape(nu, 1, ug * p)
    lam_im = pw_im[c].reshape(nu, 1, ug * p)
    return m.astype(BF16), bend(be_re), bend(be_im), cout(cl_re[1:]), cout(-cl_im[1:]), lam_re, lam_im


def _ssm_kernel(x_ref, m_ref, bre_ref, bim_ref, cre_ref, cim_ref, lre_ref, lim_ref, y_ref,
                xf_sc, zre_sc, zim_sc, yf_sc, *, n_chunk):
    c = SSM_CHUNK
    xf_sc[...] = x_ref[0].astype(F32)
    u = jnp.concatenate([xf_sc[pl.ds(s, n_chunk, stride=c), :] for s in range(c)], axis=1).astype(BF16)
    zre_sc[...] = jnp.dot(u, bre_ref[0], preferred_element_type=F32)
    zim_sc[...] = jnp.dot(u, bim_ref[0], preferred_element_type=F32)
    lre = lre_ref[0]
    lim = lim_ref[0]

    def body(kc, carry):
        xr, xi = carry
        zr = zre_sc[pl.ds(kc, 1), :]
        zi = zim_sc[pl.ds(kc, 1), :]
        zre_sc[pl.ds(kc, 1), :] = xr
        zim_sc[pl.ds(kc, 1), :] = xi
        return lre * xr - lim * xi + zr, lre * xi + lim * xr + zi

    zero = jnp.zeros((1, zre_sc.shape[1]), F32)
    lax.fori_loop(0, n_chunk, body, (zero, zero))
    y = (jnp.dot(u, m_ref[0], preferred_element_type=F32)
         + jnp.dot(zre_sc[...].astype(BF16), cre_ref[0], preferred_element_type=F32)
         + jnp.dot(zim_sc[...].astype(BF16), cim_ref[0], preferred_element_type=F32))
    for t in range(c):
        yf_sc[pl.ds(t, n_chunk, stride=c), :] = y[:, t * SSM_UNIT:(t + 1) * SSM_UNIT]
    y_ref[0] = yf_sc[...].astype(y_ref.dtype)


def _ssm_scan(p3, x_blk0, weights):
    b, l, _ = p3.shape
    m, bre, bim, cre, cim, lre, lim = weights
    nu = m.shape[0]
    lanes = m.shape[1]
    sp = bre.shape[2]
    nck = l // SSM_CHUNK
    kern = functools.partial(_ssm_kernel, n_chunk=nck)
    per_unit = lambda r, cc: pl.BlockSpec((1, r, cc), lambda u, bi: (u, 0, 0))
    return pl.pallas_call(
        kern,
        grid=(nu, b),
        in_specs=[pl.BlockSpec((1, l, SSM_UNIT), lambda u, bi: (bi, 0, x_blk0 + u)),
                  per_unit(lanes, lanes), per_unit(lanes, sp), per_unit(lanes, sp),
                  per_unit(sp, lanes), per_unit(sp, lanes), per_unit(1, sp), per_unit(1, sp)],
        out_specs=pl.BlockSpec((1, l, SSM_UNIT), lambda u, bi: (bi, 0, u)),
        out_shape=jax.ShapeDtypeStruct((b, l, nu * SSM_UNIT), BF16),
        scratch_shapes=[pltpu.VMEM((l, SSM_UNIT), F32), pltpu.VMEM((nck, sp), F32),
                        pltpu.VMEM((nck, sp), F32), pltpu.VMEM((l, SSM_UNIT), F32)],
        compiler_params=_params("parallel", "arbitrary"), name="ssm_scan",
    )(p3, m, bre, bim, cre, cim, lre, lim)


def _glu_kernel(y_ref, u_ref, g_ref, d_ref, w_ref, bg_ref, o_ref):
    y = y_ref[...].astype(F32) + d_ref[...] * u_ref[...].astype(F32)
    y = jax.nn.gelu(y)
    z = jnp.dot(y.astype(BF16), w_ref[...], preferred_element_type=F32) + bg_ref[...]
    o_ref[...] = (y * jax.nn.sigmoid(z) * _silu(g_ref[...].astype(F32))).astype(o_ref.dtype)


def _ssm_glu(y, p2, u_blk, gate_blk, d_skip, w_glu, b_glu, tm):
    t, bw = y.shape
    return pl.pallas_call(
        _glu_kernel,
        grid=(t // tm,),
        in_specs=[pl.BlockSpec((tm, bw), lambda i: (i, 0)),
                  pl.BlockSpec((tm, bw), lambda i: (i, u_blk)),
                  pl.BlockSpec((tm, bw), lambda i: (i, gate_blk)),
                  pl.BlockSpec((1, bw), lambda i: (0, 0)),
                  pl.BlockSpec((bw, bw), lambda i: (0, 0)),
                  pl.BlockSpec((1, bw), lambda i: (0, 0))],
        out_specs=pl.BlockSpec((tm, bw), lambda i: (i, 0)),
        out_shape=jax.ShapeDtypeStruct((t, bw), BF16),
        compiler_params=_params("parallel"), name="ssm_glu",
    )(y, p2, p2, d_skip.reshape(1, bw).astype(F32), w_glu.astype(BF16), b_glu.reshape(1, bw).astype(F32))


def _lru_kernel(x_ref, g_ref, cw_ref, cb_ref, wg_ref, br_ref, bi_ref, lam_ref, o_ref,
                tail_sc, h_sc, a_sc, b_sc, *, tl):
    @pl.when(pl.program_id(1) == 0)
    def _():
        tail_sc[...] = jnp.zeros_like(tail_sc)
        h_sc[...] = jnp.zeros_like(h_sc)

    x = x_ref[0].astype(F32)
    w = x.shape[1]
    xcat = jnp.concatenate([tail_sc[...], x], axis=0)
    xc = cb_ref[...] + cw_ref[CONV_WIDTH - 1:CONV_WIDTH, :] * x
    for d in range(1, CONV_WIDTH):
        xd = pltpu.roll(xcat, d, 0)[SUBLANES:, :]
        xc = xc + cw_ref[CONV_WIDTH - 1 - d:CONV_WIDTH - d, :] * xd
    tail_sc[...] = x[tl - SUBLANES:, :]

    xb = xc.astype(BF16)
    nblk = w // MXU_DIM
    gates = [jnp.dot(xb[:, kb * MXU_DIM:(kb + 1) * MXU_DIM], wg_ref[kb], preferred_element_type=F32)
             for kb in range(nblk)]
    gr = jnp.concatenate([gt[:, :MXU_DIM] for gt in gates], axis=1)
    gi = jnp.concatenate([gt[:, MXU_DIM:] for gt in gates], axis=1)
    r = jax.nn.sigmoid(gr + br_ref[...])
    i = jax.nn.sigmoid(gi + bi_ref[...])
    log_a = r * (LRU_C * _log_sigmoid(lam_ref[...]))
    a = jnp.exp(log_a)
    bx = jnp.sqrt(1.0 - a * a) * (i * xc)

    sub = lax.broadcasted_iota(jnp.int32, (tl, w), 0) % SUBLANES
    d = 1
    while d < SUBLANES:
        a_prev = pltpu.roll(a, d, 0)
        b_prev = pltpu.roll(bx, d, 0)
        use = sub >= d
        bx = jnp.where(use, a * b_prev + bx, bx)
        a = jnp.where(use, a * a_prev, a)
        d *= 2
    a_sc[...] = a
    b_sc[...] = bx

    def body(kt, h_last):
        s = pl.multiple_of(kt * SUBLANES, SUBLANES)
        ht = a_sc[pl.ds(s, SUBLANES), :] * h_last + b_sc[pl.ds(s, SUBLANES), :]
        b_sc[pl.ds(s, SUBLANES), :] = ht
        return ht[SUBLANES - 1:, :]

    h_sc[...] = lax.fori_loop(0, tl // SUBLANES, body, h_sc[...])
    o_ref[0] = (b_sc[...] * _silu(g_ref[0].astype(F32))).astype(o_ref.dtype)


def _lru(p3, x_blk, gate_blk, conv_w, conv_b, w_r, b_r, w_i, b_i, lam, tl):
    b, l, _ = p3.shape
    bw = conv_w.shape[1]
    per = MXU_DIM // LRU_BLOCK
    nblk = bw // MXU_DIM

    def tile_diag(wg):
        wg = wg.astype(F32).reshape(nblk, per, LRU_BLOCK, LRU_BLOCK)
        eye = jnp.eye(per, dtype=F32)
        return jnp.einsum('kaio,ab->kaibo', wg, eye).reshape(nblk, MXU_DIM, MXU_DIM)

    wg = jnp.concatenate([tile_diag(w_r), tile_diag(w_i)], axis=2).astype(BF16)
    vec = lambda a: a.reshape(1, bw).astype(F32)
    kern = functools.partial(_lru_kernel, tl=tl)
    full = lambda shape: pl.BlockSpec(shape, lambda bi, ti: (0,) * len(shape))
    return pl.pallas_call(
        kern,
        grid=(b, l // tl),
        in_specs=[pl.BlockSpec((1, tl, bw), lambda bi, ti: (bi, ti, x_blk)),
                  pl.BlockSpec((1, tl, bw), lambda bi, ti: (bi, ti, gate_blk)),
                  full((CONV_WIDTH, bw)), full((1, bw)), full((nblk, MXU_DIM, 2 * MXU_DIM)),
                  full((1, bw)), full((1, bw)), full((1, bw))],
        out_specs=pl.BlockSpec((1, tl, bw), lambda bi, ti: (bi, ti, 0)),
        out_shape=jax.ShapeDtypeStruct((b, l, bw), BF16),
        scratch_shapes=[pltpu.VMEM((SUBLANES, bw), F32), pltpu.VMEM((1, bw), F32),
                        pltpu.VMEM((tl, bw), F32), pltpu.VMEM((tl, bw), F32)],
        compiler_params=_params("parallel", "arbitrary"), name="rg_lru",
    )(p3, p3, conv_w.astype(F32), vec(conv_b), wg, vec(b_r), vec(b_i), vec(lam))


def _tiles(b, l):
    t = b * l
    return dict(
        norm=min(256, t),
        mm_m=min(1024, t),
        mm_n=1024,
        cum=min(512, l),
        fox=min(512, l),
        sb_q=min(512, l),
        sb_k=min(MXU_DIM, l),
        glu=min(512, t),
        lru=min(512, l),
    )


def kernel(x, g_pre, w_in, b_forget, ssm_a_re, ssm_a_im, ssm_log_dt, ssm_b_re, ssm_b_im, ssm_c_re, ssm_c_im, ssm_d, w_glu, b_glu, conv_w, conv_b, w_rgate, b_rgate, w_igate, b_igate, lru_lambda, w_out, g_post):
    b, l, d = x.shape
    depth = w_in.shape[0]
    bw = d // 4
    nh = bw // HEAD_DIM
    t = b * l
    ts = _tiles(b, l)
    assert w_in.shape[2] == 12 * bw + nh and l % SSM_CHUNK == 0

    h = x.reshape(t, d)
    u = _rmsnorm(h, g_pre[0], ts["norm"])
    for layer in range(depth):
        wl = w_in[layer]
        qscale = jnp.ones((12,), F32).at[0].set(HEAD_DIM ** -0.5).at[4].set(HEAD_DIM ** -0.5)
        w_main = jnp.concatenate([wl[:, :4 * bw], wl[:, 4 * bw + nh:]], axis=1)
        w_main = (w_main.reshape(d, 12, bw) * qscale[None, :, None]).reshape(d, 12 * bw).astype(BF16)
        w_f = jnp.pad(wl[:, 4 * bw:4 * bw + nh], ((0, 0), (0, HEAD_DIM - nh))).astype(BF16)
        b_f = jnp.pad(b_forget[layer].astype(F32), (0, HEAD_DIM - nh)).reshape(1, HEAD_DIM)
        p2, log_f = _inproj(u, w_main, w_f, b_f, ts["mm_m"], ts["mm_n"])
        p3 = p2.reshape(b, l, 12 * bw)

        cumf = _forget_cumsum(log_f.reshape(b, l, HEAD_DIM), ts["cum"])
        frow = cumf[:, :, :nh].transpose(0, 2, 1).reshape(b, nh, l // ts["fox"], ts["fox"])
        o_fox = _fox_attention(p3, cumf, frow, nh, ts["fox"])
        o_sb = _sb_attention(p3, 4 * nh, nh, ts["sb_q"], ts["sb_k"])

        ssm_w = _ssm_weights(ssm_a_re[layer], ssm_a_im[layer], ssm_log_dt[layer], ssm_b_re[layer],
                             ssm_b_im[layer], ssm_c_re[layer], ssm_c_im[layer])
        y_ssm = _ssm_scan(p3, 8 * bw // SSM_UNIT, ssm_w)
        o_ssm = _ssm_glu(y_ssm.reshape(t, bw), p2, 8, 9, ssm_d[layer], w_glu[layer], b_glu[layer], ts["glu"])

        o_lru = _lru(p3, 10, 11, conv_w[layer], conv_b[layer], w_rgate[layer], b_rgate[layer],
                     w_igate[layer], b_igate[layer], lru_lambda[layer], ts["lru"])

        y = _outproj([o_fox.reshape(t, bw), o_sb.reshape(t, bw), o_ssm, o_lru.reshape(t, bw)],
                     w_out[layer].astype(BF16), ts["mm_m"], ts["mm_n"])
        g_next = g_pre[layer + 1] if layer + 1 < depth else None
        h, u = _post(y, h, g_post[layer], g_next, ts["norm"])
    return h.reshape(b, l, d)
```

```python
import functools
import math

import jax
import jax.numpy as jnp
from jax import lax
from jax.experimental import pallas as pl
from jax.experimental.pallas import tpu as pltpu

F32 = jnp.float32
BF16 = jnp.bfloat16

HEAD_DIM = 128
SSM_GROUP = 16
SSM_STATE = 64
SSM_UNIT = 128
SSM_CHUNK = 8
LRU_BLOCK = 64
LRU_C = 8.0
CONV_WIDTH = 4
RMS_EPS = 1e-6
NEG_BIG = -1e30
LOG2E = math.log2(math.e)
SB_EXP_UNDERFLOW = 110.0
ROW_GROUP = 256
SUBLANES = 8
MXU_DIM = 256
VMEM_LIMIT = 56 * 1024 * 1024


def _params(*sem):
    return pltpu.CompilerParams(dimension_semantics=sem, vmem_limit_bytes=VMEM_LIMIT)


def _log_sigmoid(x):
    return jnp.minimum(x, 0.0) - jnp.log1p(jnp.exp(-jnp.abs(x)))


def _silu(x):
    return x * jax.nn.sigmoid(x)


def _rmsnorm_kernel(x_ref, g_ref, u_ref):
    x = x_ref[...]
    ms = jnp.mean(x * x, axis=-1, keepdims=True)
    u_ref[...] = (x * lax.rsqrt(ms + RMS_EPS) * g_ref[...]).astype(u_ref.dtype)


def _rmsnorm(x, g, tm):
    t, d = x.shape
    return pl.pallas_call(
        _rmsnorm_kernel,
        grid=(t // tm,),
        in_specs=[pl.BlockSpec((tm, d), lambda i: (i, 0)),
                  pl.BlockSpec((1, d), lambda i: (0, 0))],
        out_specs=pl.BlockSpec((tm, d), lambda i: (i, 0)),
        out_shape=jax.ShapeDtypeStruct((t, d), BF16),
        compiler_params=_params("parallel"), name="rmsnorm_first",
    )(x, g.reshape(1, d))


def _post_kernel(y_ref, h_ref, gpost_ref, gnext_ref, hn_ref, u_ref):
    y = y_ref[...]
    ms = jnp.mean(y * y, axis=-1, keepdims=True)
    hn = h_ref[...] + y * lax.rsqrt(ms + RMS_EPS) * gpost_ref[...]
    hn_ref[...] = hn
    ms2 = jnp.mean(hn * hn, axis=-1, keepdims=True)
    u_ref[...] = (hn * lax.rsqrt(ms2 + RMS_EPS) * gnext_ref[...]).astype(u_ref.dtype)


def _post_last_kernel(y_ref, h_ref, gpost_ref, hn_ref):
    y = y_ref[...]
    ms = jnp.mean(y * y, axis=-1, keepdims=True)
    hn_ref[...] = h_ref[...] + y * lax.rsqrt(ms + RMS_EPS) * gpost_ref[...]


def _post(y, h, g_post, g_next, tm):
    t, d = y.shape
    row = pl.BlockSpec((tm, d), lambda i: (i, 0))
    vec = pl.BlockSpec((1, d), lambda i: (0, 0))
    if g_next is None:
        return pl.pallas_call(
            _post_last_kernel, grid=(t // tm,),
            in_specs=[row, row, vec], out_specs=row,
            out_shape=jax.ShapeDtypeStruct((t, d), F32),
            compiler_params=_params("parallel"), name="post_norm_last",
        )(y, h, g_post.reshape(1, d)), None
    return pl.pallas_call(
        _post_kernel, grid=(t // tm,),
        in_specs=[row, row, vec, vec], out_specs=[row, row],
        out_shape=[jax.ShapeDtypeStruct((t, d), F32), jax.ShapeDtypeStruct((t, d), BF16)],
        compiler_params=_params("parallel"), name="post_norm",
    )(y, h, g_post.reshape(1, d), g_next.reshape(1, d))


def _inproj_kernel(u_ref, w_ref, wf_ref, bf_ref, p_ref, lf_ref):
    u = u_ref[...]
    p_ref[...] = jnp.dot(u, w_ref[...], preferred_element_type=F32).astype(p_ref.dtype)

    @pl.when(pl.program_id(1) == 0)
    def _():
        logit = jnp.dot(u, wf_ref[...], preferred_element_type=F32) + bf_ref[...]
        lf_ref[...] = _log_sigmoid(logit)


def _inproj(u, w, wf, bf, layer, tm, tn):
    t, d = u.shape
    n = w.shape[2]
    return pl.pallas_call(
        _inproj_kernel,
        grid=(t // tm, n // tn),
        in_specs=[pl.BlockSpec((tm, d), lambda i, j: (i, 0)),
                  pl.BlockSpec((None, d, tn), lambda i, j: (layer, 0, j)),
                  pl.BlockSpec((None, d, HEAD_DIM), lambda i, j: (layer, 0, 0)),
                  pl.BlockSpec((None, 1, HEAD_DIM), lambda i, j: (layer, 0, 0))],
        out_specs=[pl.BlockSpec((tm, tn), lambda i, j: (i, j)),
                   pl.BlockSpec((tm, HEAD_DIM), lambda i, j: (i, 0))],
        out_shape=[jax.ShapeDtypeStruct((t, n), BF16),
                   jax.ShapeDtypeStruct((t, HEAD_DIM), F32)],
        compiler_params=_params("parallel", "arbitrary"), name="in_proj",
    )(u, w, wf, bf)


def _outproj_kernel(a_ref, b_ref, c_ref, d_ref, w_ref, y_ref):
    x = jnp.concatenate([a_ref[...], b_ref[...], c_ref[...], d_ref[...]], axis=1)
    y_ref[...] = jnp.dot(x, w_ref[...], preferred_element_type=F32)


def _outproj(branches, w, layer, tm, tn):
    t, bw = branches[0].shape
    _, k, n = w.shape
    bspec = pl.BlockSpec((tm, bw), lambda i, j: (i, 0))
    return pl.pallas_call(
        _outproj_kernel,
        grid=(t // tm, n // tn),
        in_specs=[bspec, bspec, bspec, bspec, pl.BlockSpec((None, k, tn), lambda i, j: (layer, 0, j))],
        out_specs=pl.BlockSpec((tm, tn), lambda i, j: (i, j)),
        out_shape=jax.ShapeDtypeStruct((t, n), F32),
        compiler_params=_params("parallel", "arbitrary"), name="out_proj",
    )(*branches, w)


def _split3(x):
    hi = x.astype(BF16).astype(F32)
    r1 = x - hi
    mid = r1.astype(BF16).astype(F32)
    lo = (r1 - mid).astype(BF16).astype(F32)
    return hi, mid, lo


def _cumsum_kernel(lf_ref, tri_ref, qx_ref, kx_ref, carry_ref, *, n_heads):
    @pl.when(pl.program_id(1) == 0)
    def _():
        carry_ref[...] = jnp.zeros_like(carry_ref)

    hi, mid, lo = _split3(lf_ref[0])
    tri = tri_ref[...]
    c = (jnp.dot(tri, hi.astype(BF16), preferred_element_type=F32)
         + jnp.dot(tri, mid.astype(BF16), preferred_element_type=F32)
         + jnp.dot(tri, lo.astype(BF16), preferred_element_type=F32)) + carry_ref[...]
    carry_ref[...] = c[-1:, :]

    c2 = c * LOG2E
    lane = lax.broadcasted_iota(jnp.int32, c2.shape, 1)
    ones = jnp.where(lane < 3, 1.0, 0.0)
    for h in range(n_heads):
        f = jnp.sum(jnp.where(lane == h, c2, 0.0), axis=1, keepdims=True)
        hi, mid, lo = _split3(f)
        pieces = jnp.where(lane == 0, hi, jnp.where(lane == 1, mid, jnp.where(lane == 2, lo, 0.0)))
        qx_ref[0, h] = (pieces + pltpu.roll(ones, 3, 1)).astype(qx_ref.dtype)
        kx_ref[0, h] = (ones - pltpu.roll(pieces, 3, 1)).astype(kx_ref.dtype)


def _forget_columns(lf, n_heads, tc):
    b, l, w = lf.shape
    tri = jnp.tril(jnp.ones((tc, tc), F32)).astype(BF16)
    out = jax.ShapeDtypeStruct((b, n_heads, l, w), BF16)
    ospec = pl.BlockSpec((1, n_heads, tc, w), lambda i, j: (i, 0, j, 0))
    return pl.pallas_call(
        functools.partial(_cumsum_kernel, n_heads=n_heads),
        grid=(b, l // tc),
        in_specs=[pl.BlockSpec((1, tc, w), lambda i, j: (i, j, 0)),
                  pl.BlockSpec((tc, tc), lambda i, j: (0, 0))],
        out_specs=[ospec, ospec],
        out_shape=[out, out],
        scratch_shapes=[pltpu.VMEM((1, w), F32)],
        compiler_params=_params("parallel", "arbitrary"), name="forget_cumsum",
    )(lf, tri)


def _fox_kernel(q_ref, qx_ref, k_ref, kx_ref, v_ref, g_ref, o_ref, s_sc, m_sc, acc_sc, *, tq, rs):
    qi = pl.program_id(2)
    m_sc[...] = jnp.full_like(m_sc, NEG_BIG)
    acc_sc[...] = jnp.zeros_like(acc_sc)
    nrep = tq // HEAD_DIM
    ones = jnp.ones((tq, HEAD_DIM), BF16)
    contract_last = (((1,), (1,)), ((), ()))
    row_groups = [pl.ds(r0, rs) for r0 in range(0, tq, rs)]

    def scores(kb, slot):
        start = pl.multiple_of(kb * tq, tq)
        k = jnp.concatenate([k_ref[0, pl.ds(start, tq), :], kx_ref[0, 0, pl.ds(start, tq), :]], axis=1)
        for rows in row_groups:
            q = jnp.concatenate([q_ref[0, rows, :], qx_ref[0, 0, rows, :]], axis=1)
            s_sc[slot, rows, :] = lax.dot_general(q, k, contract_last, preferred_element_type=F32)

    def accumulate(kb, slot):
        start = pl.multiple_of(kb * tq, tq)
        v = jnp.concatenate([v_ref[0, pl.ds(start, tq), :], ones], axis=1)
        visible = (qi - kb) * tq
        for rows in row_groups:
            ahead = (lax.broadcasted_iota(jnp.int32, (rs, tq), 1)
                     - lax.broadcasted_iota(jnp.int32, (rs, tq), 0) - rows.start)
            s = jnp.where(ahead <= visible, s_sc[slot, rows, :], NEG_BIG)
            m_prev = m_sc[rows, :]
            m_new = jnp.maximum(m_prev, jnp.max(s, axis=1, keepdims=True))
            alpha = jnp.exp2(m_prev - m_new)
            p = jnp.exp2(s - jnp.tile(m_new, (1, nrep)))
            acc_sc[rows, :] = (jnp.tile(alpha, (1, 2)) * acc_sc[rows, :]
                               + jnp.dot(p.astype(BF16), v, preferred_element_type=F32))
            m_sc[rows, :] = m_new

    scores(0, 0)

    def pair(j, carry):
        kb = 2 * j
        scores(kb + 1, 1)
        accumulate(kb, 0)
        scores(jnp.minimum(kb + 2, qi), 0)
        accumulate(kb + 1, 1)
        return carry

    n_blocks = qi + 1
    lax.fori_loop(0, n_blocks // 2, pair, 0)

    @pl.when(n_blocks % 2 == 1)
    def _():
        accumulate(qi, 0)

    gate = g_ref[0].astype(F32)
    o_ref[0] = (acc_sc[:, :HEAD_DIM] / acc_sc[:, HEAD_DIM:] * _silu(gate)).astype(o_ref.dtype)


def _fox_attention(p3, qx, kx, n_heads, tq):
    b, l, _ = p3.shape
    nq = l // tq
    kern = functools.partial(_fox_kernel, tq=tq, rs=min(ROW_GROUP, tq))
    return pl.pallas_call(
        kern,
        grid=(b, n_heads, nq),
        in_specs=[pl.BlockSpec((1, tq, HEAD_DIM), lambda bi, h, qi: (bi, qi, h)),
                  pl.BlockSpec((1, 1, tq, HEAD_DIM), lambda bi, h, qi: (bi, h, qi, 0)),
                  pl.BlockSpec((1, l, HEAD_DIM), lambda bi, h, qi: (bi, 0, n_heads + h)),
                  pl.BlockSpec((1, 1, l, HEAD_DIM), lambda bi, h, qi: (bi, h, 0, 0)),
                  pl.BlockSpec((1, l, HEAD_DIM), lambda bi, h, qi: (bi, 0, 2 * n_heads + h)),
                  pl.BlockSpec((1, tq, HEAD_DIM), lambda bi, h, qi: (bi, qi, 3 * n_heads + h))],
        out_specs=pl.BlockSpec((1, tq, HEAD_DIM), lambda bi, h, qi: (bi, qi, h)),
        out_shape=jax.ShapeDtypeStruct((b, l, n_heads * HEAD_DIM), BF16),
        scratch_shapes=[pltpu.VMEM((2, tq, tq), F32), pltpu.VMEM((tq, HEAD_DIM), F32),
                        pltpu.VMEM((tq, 2 * HEAD_DIM), F32)],
        compiler_params=_params("parallel", "parallel", "arbitrary"), name="fox_attn",
    )(p3, qx, p3, kx, p3, p3)


def _sb_kernel(q_ref, k_ref, v_ref, g_ref, tri_ref, o_ref, cost_sc, acc_sc, *, tq, tk, rs):
    qi = pl.program_id(2)
    nd = tq // tk
    cost_sc[...] = jnp.zeros_like(cost_sc)
    acc_sc[...] = jnp.zeros_like(acc_sc)

    def step(kb, diagonal):
        start = pl.multiple_of(kb * tk, tk)
        k = k_ref[0, pl.ds(start, tk), :]
        v = v_ref[0, pl.ds(start, tk), :]
        for r0 in range(0, tq, rs):
            rows = pl.ds(r0, rs)
            z = lax.dot_general(q_ref[0, rows, :], k, (((1,), (1,)), ((), ())), preferred_element_type=F32)
            softplus = jnp.maximum(z, 0.0) + jnp.log(1.0 + jnp.exp(-jnp.abs(z)))
            log_beta = z - softplus
            if diagonal:
                row = lax.broadcasted_iota(jnp.int32, (rs, tk), 0) + (qi * tq + r0)
                col = lax.broadcasted_iota(jnp.int32, (rs, tk), 1) + kb * tk
                mask = col < row
                softplus = jnp.where(mask, softplus, 0.0)
            within = jnp.dot(softplus.astype(BF16), tri_ref[...], preferred_element_type=F32)
            w = jnp.exp(log_beta - within - jnp.tile(cost_sc[rows, :], (1, tk // HEAD_DIM)))
            if diagonal:
                w = jnp.where(mask, w, 0.0)
            acc_sc[rows, :] += jnp.dot(w.astype(BF16), v, preferred_element_type=F32)
            cost_sc[rows, :] += jnp.sum(softplus, axis=1, keepdims=True)

    for j in reversed(range(nd)):
        step(qi * nd + j, True)

    def cond(state):
        return jnp.logical_and(state[0] >= 0, state[1] < SB_EXP_UNDERFLOW)

    def body(state):
        step(state[0], False)
        return state[0] - 1, jnp.min(cost_sc[...])

    lax.while_loop(cond, body, (qi * nd - 1, jnp.min(cost_sc[...])))
    gate = g_ref[0].astype(F32)
    o_ref[0] = (acc_sc[...] * _silu(gate)).astype(o_ref.dtype)


def _sb_attention(p3, col0, n_heads, tq, tk):
    b, l, _ = p3.shape
    nq = l // tq
    tri = jnp.tril(jnp.ones((tk, tk), F32), k=-1).astype(BF16)
    kern = functools.partial(_sb_kernel, tq=tq, tk=tk, rs=min(ROW_GROUP, tq))
    return pl.pallas_call(
        kern,
        grid=(b, n_heads, nq),
        in_specs=[pl.BlockSpec((1, tq, HEAD_DIM), lambda bi, h, qi: (bi, qi, col0 + h)),
                  pl.BlockSpec((1, l, HEAD_DIM), lambda bi, h, qi: (bi, 0, col0 + n_heads + h)),
                  pl.BlockSpec((1, l, HEAD_DIM), lambda bi, h, qi: (bi, 0, col0 + 2 * n_heads + h)),
                  pl.BlockSpec((1, tq, HEAD_DIM), lambda bi, h, qi: (bi, qi, col0 + 3 * n_heads + h)),
                  pl.BlockSpec((tk, tk), lambda bi, h, qi: (0, 0))],
        out_specs=pl.BlockSpec((1, tq, HEAD_DIM), lambda bi, h, qi: (bi, qi, h)),
        out_shape=jax.ShapeDtypeStruct((b, l, n_heads * HEAD_DIM), BF16),
        scratch_shapes=[pltpu.VMEM((tq, HEAD_DIM), F32)] * 2,
        compiler_params=_params("parallel", "parallel", "arbitrary"), name="sb_attn",
    )(p3, p3, p3, p3, tri)


def _ssm_weights(a_re, a_im, log_dt, b_re, b_im, c_re, c_im):
    g, p = a_re.shape
    hg = b_re.shape[-1]
    c = SSM_CHUNK
    dt = jnp.exp(log_dt.astype(F32))[:, None]
    a_re = a_re.astype(F32)
    a_im = a_im.astype(F32)
    n = jnp.arange(c + 1, dtype=F32)[:, None, None]
    mag = jnp.exp(a_re * dt * n)
    ang = a_im * dt * n
    pw_re, pw_im = mag * jnp.cos(ang), mag * jnp.sin(ang)
    num_re, num_im = pw_re[1] - 1.0, pw_im[1]
    den = a_re * a_re + a_im * a_im
    f_re = (num_re * a_re + num_im * a_im) / den
    f_im = (num_im * a_re - num_re * a_im) / den
    bb_re = f_re[..., None] * b_re - f_im[..., None] * b_im
    bb_im = f_re[..., None] * b_im + f_im[..., None] * b_re
    cc_re, cc_im = c_re.astype(F32), c_im.astype(F32)
    hi = lax.Precision.HIGHEST
    cl_re = cc_re[None] * pw_re[:, :, None, :] - cc_im[None] * pw_im[:, :, None, :]
    cl_im = cc_re[None] * pw_im[:, :, None, :] + cc_im[None] * pw_re[:, :, None, :]
    kk = (jnp.einsum('ngop,gpi->ngoi', cl_re[:c], bb_re, precision=hi)
          - jnp.einsum('ngop,gpi->ngoi', cl_im[:c], bb_im, precision=hi))
    ug = SSM_UNIT // hg
    nu = g // ug
    eye = jnp.eye(ug, dtype=F32)
    bd = jnp.einsum('nugoi,gh->nugiho', kk.reshape(c, nu, ug, hg, hg), eye)
    bd = bd.reshape(c, nu, SSM_UNIT, SSM_UNIT).astype(BF16)
    zero = jnp.zeros_like(bd[0])
    m = jnp.concatenate([jnp.concatenate([bd[t - s] if t >= s else zero for t in range(c)], axis=2)
                         for s in range(c)], axis=1)
    n_rev = jnp.arange(c - 1, -1, -1, dtype=F32)[:, None, None]
    pr = jnp.exp(a_re * dt * n_rev) * jnp.cos(a_im * dt * n_rev)
    pi_ = jnp.exp(a_re * dt * n_rev) * jnp.sin(a_im * dt * n_rev)
    be_re = pr[..., None] * bb_re[None] - pi_[..., None] * bb_im[None]
    be_im = pr[..., None] * bb_im[None] + pi_[..., None] * bb_re[None]

    def bend(x):
        x = jnp.einsum('sugpi,gh->usgihp', x.reshape(c, nu, ug, p, hg), eye)
        return x.reshape(nu, c * SSM_UNIT, ug * p).astype(BF16)

    def cout(x):
        x = jnp.einsum('tugop,gh->tugpho', x.reshape(c, nu, ug, hg, p), eye)
        x = x.reshape(c, nu, ug * p, SSM_UNIT).astype(BF16)
        return jnp.concatenate([x[t] for t in range(c)], axis=2)

    lam_re = pw_re[c].reshape(nu, 1, ug * p)
    lam_im = pw_im[c].reshape(nu, 1, ug * p)
    return m, bend(be_re), bend(be_im), cout(cl_re[1:]), cout(-cl_im[1:]), lam_re, lam_im


def _ssm_kernel(x_ref, m_ref, bre_ref, bim_ref, cre_ref, cim_ref, lre_ref, lim_ref, y_ref,
                xf_sc, zre_sc, zim_sc, yf_sc, *, n_chunk):
    c = SSM_CHUNK
    xf_sc[...] = x_ref[0].astype(F32)
    u = jnp.concatenate([xf_sc[pl.ds(s, n_chunk, stride=c), :] for s in range(c)], axis=1).astype(BF16)
    zre_sc[...] = jnp.dot(u, bre_ref[0], preferred_element_type=F32)
    zim_sc[...] = jnp.dot(u, bim_ref[0], preferred_element_type=F32)
    lre = lre_ref[0]
    lim = lim_ref[0]

    def body(kc, carry):
        xr, xi = carry
        zr = zre_sc[pl.ds(kc, 1), :]
        zi = zim_sc[pl.ds(kc, 1), :]
        zre_sc[pl.ds(kc, 1), :] = xr
        zim_sc[pl.ds(kc, 1), :] = xi
        return lre * xr - lim * xi + zr, lre * xi + lim * xr + zi

    zero = jnp.zeros((1, zre_sc.shape[1]), F32)
    lax.fori_loop(0, n_chunk, body, (zero, zero))
    y = (jnp.dot(u, m_ref[0], preferred_element_type=F32)
         + jnp.dot(zre_sc[...].astype(BF16), cre_ref[0], preferred_element_type=F32)
         + jnp.dot(zim_sc[...].astype(BF16), cim_ref[0], preferred_element_type=F32))
    for t in range(c):
        yf_sc[pl.ds(t, n_chunk, stride=c), :] = y[:, t * SSM_UNIT:(t + 1) * SSM_UNIT]
    y_ref[0] = yf_sc[...].astype(y_ref.dtype)


def _ssm_scan(p3, x_blk0, weights, layer):
    b, l, _ = p3.shape
    m, bre, bim, cre, cim, lre, lim = weights
    nu = m.shape[1]
    lanes = m.shape[2]
    sp = bre.shape[3]
    nck = l // SSM_CHUNK
    kern = functools.partial(_ssm_kernel, n_chunk=nck)
    per_unit = lambda r, cc: pl.BlockSpec((None, 1, r, cc), lambda u, bi: (layer, u, 0, 0))
    return pl.pallas_call(
        kern,
        grid=(nu, b),
        in_specs=[pl.BlockSpec((1, l, SSM_UNIT), lambda u, bi: (bi, 0, x_blk0 + u)),
                  per_unit(lanes, lanes), per_unit(lanes, sp), per_unit(lanes, sp),
                  per_unit(sp, lanes), per_unit(sp, lanes), per_unit(1, sp), per_unit(1, sp)],
        out_specs=pl.BlockSpec((1, l, SSM_UNIT), lambda u, bi: (bi, 0, u)),
        out_shape=jax.ShapeDtypeStruct((b, l, nu * SSM_UNIT), BF16),
        scratch_shapes=[pltpu.VMEM((l, SSM_UNIT), F32), pltpu.VMEM((nck, sp), F32),
                        pltpu.VMEM((nck, sp), F32), pltpu.VMEM((l, SSM_UNIT), F32)],
        compiler_params=_params("parallel", "arbitrary"), name="ssm_scan",
    )(p3, m, bre, bim, cre, cim, lre, lim)


def _glu_kernel(y_ref, u_ref, g_ref, d_ref, w_ref, bg_ref, o_ref):
    y = y_ref[...].astype(F32) + d_ref[...] * u_ref[...].astype(F32)
    y = jax.nn.gelu(y)
    z = jnp.dot(y.astype(BF16), w_ref[...], preferred_element_type=F32) + bg_ref[...]
    o_ref[...] = (y * jax.nn.sigmoid(z) * _silu(g_ref[...].astype(F32))).astype(o_ref.dtype)


def _ssm_glu(y, p2, u_blk, gate_blk, d_skip, w_glu, b_glu, layer, tm):
    t, bw = y.shape
    return pl.pallas_call(
        _glu_kernel,
        grid=(t // tm,),
        in_specs=[pl.BlockSpec((tm, bw), lambda i: (i, 0)),
                  pl.BlockSpec((tm, bw), lambda i: (i, u_blk)),
                  pl.BlockSpec((tm, bw), lambda i: (i, gate_blk)),
                  pl.BlockSpec((1, bw), lambda i: (0, 0)),
                  pl.BlockSpec((None, bw, bw), lambda i: (layer, 0, 0)),
                  pl.BlockSpec((1, bw), lambda i: (0, 0))],
        out_specs=pl.BlockSpec((tm, bw), lambda i: (i, 0)),
        out_shape=jax.ShapeDtypeStruct((t, bw), BF16),
        compiler_params=_params("parallel"), name="ssm_glu",
    )(y, p2, p2, d_skip.reshape(1, bw).astype(F32), w_glu, b_glu.reshape(1, bw).astype(F32))


def _lru_kernel(x_ref, g_ref, cw_ref, cb_ref, wg_ref, br_ref, bi_ref, lam_ref, o_ref,
                tail_sc, h_sc, a_sc, b_sc, *, tl):
    @pl.when(pl.program_id(1) == 0)
    def _():
        tail_sc[...] = jnp.zeros_like(tail_sc)
        h_sc[...] = jnp.zeros_like(h_sc)

    x = x_ref[0].astype(F32)
    w = x.shape[1]
    xcat = jnp.concatenate([tail_sc[...], x], axis=0)
    xc = cb_ref[...] + cw_ref[CONV_WIDTH - 1:CONV_WIDTH, :] * x
    for d in range(1, CONV_WIDTH):
        xd = pltpu.roll(xcat, d, 0)[SUBLANES:, :]
        xc = xc + cw_ref[CONV_WIDTH - 1 - d:CONV_WIDTH - d, :] * xd
    tail_sc[...] = x[tl - SUBLANES:, :]

    xb = xc.astype(BF16)
    nblk = w // MXU_DIM
    gates = [jnp.dot(xb[:, kb * MXU_DIM:(kb + 1) * MXU_DIM], wg_ref[kb], preferred_element_type=F32)
             for kb in range(nblk)]
    gr = jnp.concatenate([gt[:, :MXU_DIM] for gt in gates], axis=1)
    gi = jnp.concatenate([gt[:, MXU_DIM:] for gt in gates], axis=1)
    r = jax.nn.sigmoid(gr + br_ref[...])
    i = jax.nn.sigmoid(gi + bi_ref[...])
    log_a = r * (LRU_C * _log_sigmoid(lam_ref[...]))
    a = jnp.exp(log_a)
    bx = jnp.sqrt(1.0 - a * a) * (i * xc)

    sub = lax.broadcasted_iota(jnp.int32, (tl, w), 0) % SUBLANES
    d = 1
    while d < SUBLANES:
        a_prev = pltpu.roll(a, d, 0)
        b_prev = pltpu.roll(bx, d, 0)
        use = sub >= d
        bx = jnp.where(use, a * b_prev + bx, bx)
        a = jnp.where(use, a * a_prev, a)
        d *= 2
    a_sc[...] = a
    b_sc[...] = bx

    def body(kt, h_last):
        s = pl.multiple_of(kt * SUBLANES, SUBLANES)
        ht = a_sc[pl.ds(s, SUBLANES), :] * h_last + b_sc[pl.ds(s, SUBLANES), :]
        b_sc[pl.ds(s, SUBLANES), :] = ht
        return ht[SUBLANES - 1:, :]

    h_sc[...] = lax.fori_loop(0, tl // SUBLANES, body, h_sc[...])
    o_ref[0] = (b_sc[...] * _silu(g_ref[0].astype(F32))).astype(o_ref.dtype)


def _lru_gate_tiles(w_r, w_i):
    depth, nb = w_r.shape[:2]
    per = MXU_DIM // LRU_BLOCK
    nblk = nb // per

    def tile_diag(wg):
        wg = wg.astype(F32).reshape(depth, nblk, per, LRU_BLOCK, LRU_BLOCK)
        eye = jnp.eye(per, dtype=F32)
        return jnp.einsum('dkaio,ab->dkaibo', wg, eye).reshape(depth, nblk, MXU_DIM, MXU_DIM)

    return jnp.concatenate([tile_diag(w_r), tile_diag(w_i)], axis=3).astype(BF16)


def _lru(p3, x_blk, gate_blk, conv_w, conv_b, wg, b_r, b_i, lam, layer, tl):
    b, l, _ = p3.shape
    bw = conv_w.shape[1]
    nblk = bw // MXU_DIM
    vec = lambda a: a.reshape(1, bw).astype(F32)
    kern = functools.partial(_lru_kernel, tl=tl)
    full = lambda shape: pl.BlockSpec(shape, lambda bi, ti: (0,) * len(shape))
    return pl.pallas_call(
        kern,
        grid=(b, l // tl),
        in_specs=[pl.BlockSpec((1, tl, bw), lambda bi, ti: (bi, ti, x_blk)),
                  pl.BlockSpec((1, tl, bw), lambda bi, ti: (bi, ti, gate_blk)),
                  full((CONV_WIDTH, bw)), full((1, bw)),
                  pl.BlockSpec((None, nblk, MXU_DIM, 2 * MXU_DIM), lambda bi, ti: (layer, 0, 0, 0)),
                  full((1, bw)), full((1, bw)), full((1, bw))],
        out_specs=pl.BlockSpec((1, tl, bw), lambda bi, ti: (bi, ti, 0)),
        out_shape=jax.ShapeDtypeStruct((b, l, bw), BF16),
        scratch_shapes=[pltpu.VMEM((SUBLANES, bw), F32), pltpu.VMEM((1, bw), F32),
                        pltpu.VMEM((tl, bw), F32), pltpu.VMEM((tl, bw), F32)],
        compiler_params=_params("parallel", "arbitrary"), name="rg_lru",
    )(p3, p3, conv_w.astype(F32), vec(conv_b), wg, vec(b_r), vec(b_i), vec(lam))


def _tiles(b, l):
    t = b * l
    return dict(
        norm=min(256, t),
        mm_m=min(1024, t),
        mm_n=1024,
        cum=min(512, l),
        fox=min(512, l),
        sb_q=min(512, l),
        sb_k=min(MXU_DIM, l),
        glu=min(512, t),
        lru=min(512, l),
    )


def kernel(x, g_pre, w_in, b_forget, ssm_a_re, ssm_a_im, ssm_log_dt, ssm_b_re, ssm_b_im, ssm_c_re, ssm_c_im, ssm_d, w_glu, b_glu, conv_w, conv_b, w_rgate, b_rgate, w_igate, b_igate, lru_lambda, w_out, g_post):
    b, l, d = x.shape
    depth = w_in.shape[0]
    bw = d // 4
    nh = bw // HEAD_DIM
    t = b * l
    ts = _tiles(b, l)
    assert w_in.shape[2] == 12 * bw + nh and l % SSM_CHUNK == 0

    qscale = jnp.ones((12,), F32).at[0].set(HEAD_DIM ** -0.5 * LOG2E).at[4].set(HEAD_DIM ** -0.5)
    w_main = jnp.concatenate([w_in[:, :, :4 * bw], w_in[:, :, 4 * bw + nh:]], axis=2)
    w_main = (w_main.reshape(depth, d, 12, bw) * qscale[None, None, :, None]).reshape(depth, d, 12 * bw).astype(BF16)
    w_f = jnp.pad(w_in[:, :, 4 * bw:4 * bw + nh], ((0, 0), (0, 0), (0, HEAD_DIM - nh))).astype(BF16)
    b_f = jnp.pad(b_forget.astype(F32), ((0, 0), (0, HEAD_DIM - nh))).reshape(depth, 1, HEAD_DIM)
    ssm_w = jax.vmap(_ssm_weights)(ssm_a_re, ssm_a_im, ssm_log_dt, ssm_b_re, ssm_b_im, ssm_c_re, ssm_c_im)
    lru_wg = _lru_gate_tiles(w_rgate, w_igate)
    w_glu_b = w_glu.astype(BF16)
    w_out_b = w_out.astype(BF16)

    h = x.reshape(t, d)
    u = _rmsnorm(h, g_pre[0], ts["norm"])
    for layer in range(depth):
        p2, log_f = _inproj(u, w_main, w_f, b_f, layer, ts["mm_m"], ts["mm_n"])
        p3 = p2.reshape(b, l, 12 * bw)

        qx, kx = _forget_columns(log_f.reshape(b, l, HEAD_DIM), nh, ts["cum"])
        o_fox = _fox_attention(p3, qx, kx, nh, ts["fox"])
        o_sb = _sb_attention(p3, 4 * nh, nh, ts["sb_q"], ts["sb_k"])

        y_ssm = _ssm_scan(p3, 8 * bw // SSM_UNIT, ssm_w, layer)
        o_ssm = _ssm_glu(y_ssm.reshape(t, bw), p2, 8, 9, ssm_d[layer], w_glu_b, b_glu[layer], layer, ts["glu"])

        o_lru = _lru(p3, 10, 11, conv_w[layer], conv_b[layer], lru_wg, b_rgate[layer], b_igate[layer],
                     lru_lambda[layer], layer, ts["lru"])

        y = _outproj([o_fox.reshape(t, bw), o_sb.reshape(t, bw), o_ssm, o_lru.reshape(t, bw)],
                     w_out_b, layer, ts["mm_m"], ts["mm_n"])
        g_next = g_pre[layer + 1] if layer + 1 < depth else None
        h, u = _post(y, h, g_post[layer], g_next, ts["norm"])
    return h.reshape(b, l, d)
```

```python
import functools
import math

import jax
import jax.numpy as jnp
from jax import lax
from jax.experimental import pallas as pl
from jax.experimental.pallas import tpu as pltpu

F32 = jnp.float32
BF16 = jnp.bfloat16

HEAD_DIM = 128
SSM_GROUP = 16
SSM_STATE = 64
SSM_UNIT = 128
SSM_CHUNK = 8
LRU_BLOCK = 64
LRU_C = 8.0
CONV_WIDTH = 4
RMS_EPS = 1e-6
NEG_BIG = -1e30
LOG2E = math.log2(math.e)
TINY = 1e-37
SB_EXP_UNDERFLOW = 110.0
ROW_GROUP = 256
SUBLANES = 8
MXU_DIM = 256
VMEM_LIMIT = 56 * 1024 * 1024


def _params(*sem):
    return pltpu.CompilerParams(dimension_semantics=sem, vmem_limit_bytes=VMEM_LIMIT)


def _log_sigmoid(x):
    return jnp.minimum(x, 0.0) - jnp.log1p(jnp.exp(-jnp.abs(x)))


def _silu(x):
    return x * jax.nn.sigmoid(x)


def _rmsnorm_kernel(x_ref, g_ref, u_ref):
    x = x_ref[...]
    ms = jnp.mean(x * x, axis=-1, keepdims=True)
    u_ref[...] = (x * lax.rsqrt(ms + RMS_EPS) * g_ref[...]).astype(u_ref.dtype)


def _rmsnorm(x, g, tm):
    t, d = x.shape
    return pl.pallas_call(
        _rmsnorm_kernel,
        grid=(t // tm,),
        in_specs=[pl.BlockSpec((tm, d), lambda i: (i, 0)),
                  pl.BlockSpec((1, d), lambda i: (0, 0))],
        out_specs=pl.BlockSpec((tm, d), lambda i: (i, 0)),
        out_shape=jax.ShapeDtypeStruct((t, d), BF16),
        compiler_params=_params("parallel"), name="rmsnorm_first",
    )(x, g.reshape(1, d))


def _post_kernel(y_ref, h_ref, gpost_ref, gnext_ref, hn_ref, u_ref):
    y = y_ref[...].astype(F32)
    ms = jnp.mean(y * y, axis=-1, keepdims=True)
    hn = h_ref[...] + y * lax.rsqrt(ms + RMS_EPS) * gpost_ref[...]
    hn_ref[...] = hn
    ms2 = jnp.mean(hn * hn, axis=-1, keepdims=True)
    u_ref[...] = (hn * lax.rsqrt(ms2 + RMS_EPS) * gnext_ref[...]).astype(u_ref.dtype)


def _post_last_kernel(y_ref, h_ref, gpost_ref, hn_ref):
    y = y_ref[...].astype(F32)
    ms = jnp.mean(y * y, axis=-1, keepdims=True)
    hn_ref[...] = h_ref[...] + y * lax.rsqrt(ms + RMS_EPS) * gpost_ref[...]


def _post(y, h, g_post, g_next, tm):
    t, d = y.shape
    row = pl.BlockSpec((tm, d), lambda i: (i, 0))
    vec = pl.BlockSpec((1, d), lambda i: (0, 0))
    if g_next is None:
        return pl.pallas_call(
            _post_last_kernel, grid=(t // tm,),
            in_specs=[row, row, vec], out_specs=row,
            out_shape=jax.ShapeDtypeStruct((t, d), F32),
            compiler_params=_params("parallel"), name="post_norm_last",
        )(y, h, g_post.reshape(1, d)), None
    return pl.pallas_call(
        _post_kernel, grid=(t // tm,),
        in_specs=[row, row, vec, vec], out_specs=[row, row],
        out_shape=[jax.ShapeDtypeStruct((t, d), F32), jax.ShapeDtypeStruct((t, d), BF16)],
        compiler_params=_params("parallel"), name="post_norm",
    )(y, h, g_post.reshape(1, d), g_next.reshape(1, d))


def _inproj_kernel(u_ref, w_ref, wf_ref, bf_ref, p_ref, lf_ref):
    u = u_ref[...]
    p_ref[...] = jnp.dot(u, w_ref[...], preferred_element_type=F32).astype(p_ref.dtype)

    @pl.when(pl.program_id(1) == 0)
    def _():
        logit = jnp.dot(u, wf_ref[...], preferred_element_type=F32) + bf_ref[...]
        lf_ref[...] = _log_sigmoid(logit)


def _matmul_kernel(u_ref, w_ref, p_ref):
    p_ref[...] = jnp.dot(u_ref[...], w_ref[...], preferred_element_type=F32).astype(p_ref.dtype)


def _inproj_plain(u, w, layer, tm, tn):
    t, d = u.shape
    n = w.shape[2]
    return pl.pallas_call(
        _matmul_kernel,
        grid=(t // tm, n // tn),
        in_specs=[pl.BlockSpec((tm, d), lambda i, j: (i, 0)),
                  pl.BlockSpec((None, d, tn), lambda i, j: (layer, 0, j))],
        out_specs=pl.BlockSpec((tm, tn), lambda i, j: (i, j)),
        out_shape=jax.ShapeDtypeStruct((t, n), BF16),
        compiler_params=_params("parallel", "arbitrary"), name="in_proj_rest",
    )(u, w)


def _inproj(u, w, wf, bf, layer, tm, tn):
    t, d = u.shape
    n = w.shape[2]
    return pl.pallas_call(
        _inproj_kernel,
        grid=(t // tm, n // tn),
        in_specs=[pl.BlockSpec((tm, d), lambda i, j: (i, 0)),
                  pl.BlockSpec((None, d, tn), lambda i, j: (layer, 0, j)),
                  pl.BlockSpec((None, d, HEAD_DIM), lambda i, j: (layer, 0, 0)),
                  pl.BlockSpec((None, 1, HEAD_DIM), lambda i, j: (layer, 0, 0))],
        out_specs=[pl.BlockSpec((tm, tn), lambda i, j: (i, j)),
                   pl.BlockSpec((tm, HEAD_DIM), lambda i, j: (i, 0))],
        out_shape=[jax.ShapeDtypeStruct((t, n), BF16),
                   jax.ShapeDtypeStruct((t, HEAD_DIM), F32)],
        compiler_params=_params("parallel", "arbitrary"), name="in_proj",
    )(u, w, wf, bf)


def _outproj_kernel(a_ref, b_ref, c_ref, d_ref, w_ref, y_ref):
    x = jnp.concatenate([a_ref[...], b_ref[...], c_ref[...], d_ref[...]], axis=1)
    y_ref[...] = jnp.dot(x, w_ref[...], preferred_element_type=F32).astype(y_ref.dtype)


def _outproj(branches, w, layer, tm, tn):
    t, bw = branches[0].shape
    _, k, n = w.shape
    bspec = pl.BlockSpec((tm, bw), lambda i, j: (i, 0))
    return pl.pallas_call(
        _outproj_kernel,
        grid=(t // tm, n // tn),
        in_specs=[bspec, bspec, bspec, bspec, pl.BlockSpec((None, k, tn), lambda i, j: (layer, 0, j))],
        out_specs=pl.BlockSpec((tm, tn), lambda i, j: (i, j)),
        out_shape=jax.ShapeDtypeStruct((t, n), BF16),
        compiler_params=_params("parallel", "arbitrary"), name="out_proj",
    )(*branches, w)


def _split3(x):
    hi = x.astype(BF16).astype(F32)
    r1 = x - hi
    mid = r1.astype(BF16).astype(F32)
    lo = (r1 - mid).astype(BF16).astype(F32)
    return hi, mid, lo


def _cumsum_kernel(lf_ref, tri_ref, qx_ref, kx_ref, carry_ref, *, n_heads):
    @pl.when(pl.program_id(1) == 0)
    def _():
        carry_ref[...] = jnp.zeros_like(carry_ref)

    hi, mid, lo = _split3(lf_ref[0])
    tri = tri_ref[...]
    c = (jnp.dot(tri, hi.astype(BF16), preferred_element_type=F32)
         + jnp.dot(tri, mid.astype(BF16), preferred_element_type=F32)
         + jnp.dot(tri, lo.astype(BF16), preferred_element_type=F32)) + carry_ref[...]
    carry_ref[...] = c[-1:, :]

    c2 = c * LOG2E
    lane = lax.broadcasted_iota(jnp.int32, c2.shape, 1)
    ones = jnp.where(lane < 3, 1.0, 0.0)
    for h in range(n_heads):
        f = jnp.sum(jnp.where(lane == h, c2, 0.0), axis=1, keepdims=True)
        hi, mid, lo = _split3(f)
        pieces = jnp.where(lane == 0, hi, jnp.where(lane == 1, mid, jnp.where(lane == 2, lo, 0.0)))
        qx_ref[0, h] = (pieces + pltpu.roll(ones, 3, 1)).astype(qx_ref.dtype)
        kx_ref[0, h] = (ones - pltpu.roll(pieces, 3, 1)).astype(kx_ref.dtype)


def _forget_columns(lf, n_heads, tc):
    b, l, w = lf.shape
    tri = jnp.tril(jnp.ones((tc, tc), F32)).astype(BF16)
    out = jax.ShapeDtypeStruct((b, n_heads, l, w), BF16)
    ospec = pl.BlockSpec((1, n_heads, tc, w), lambda i, j: (i, 0, j, 0))
    return pl.pallas_call(
        functools.partial(_cumsum_kernel, n_heads=n_heads),
        grid=(b, l // tc),
        in_specs=[pl.BlockSpec((1, tc, w), lambda i, j: (i, j, 0)),
                  pl.BlockSpec((tc, tc), lambda i, j: (0, 0))],
        out_specs=[ospec, ospec],
        out_shape=[out, out],
        scratch_shapes=[pltpu.VMEM((1, w), F32)],
        compiler_params=_params("parallel", "arbitrary"), name="forget_cumsum",
    )(lf, tri)


def _fox_kernel(q_ref, qx_ref, k_ref, kx_ref, v_ref, g_ref, o_ref, s_sc, m_sc, acc_sc, *, tq, rs):
    qi = pl.program_id(2)
    m_sc[...] = jnp.full_like(m_sc, NEG_BIG)
    acc_sc[...] = jnp.zeros_like(acc_sc)
    nrep = tq // HEAD_DIM
    ones = jnp.ones((tq, HEAD_DIM), BF16)
    contract_last = (((1,), (1,)), ((), ()))
    row_groups = [pl.ds(r0, rs) for r0 in range(0, tq, rs)]

    def scores(kb, slot):
        start = pl.multiple_of(kb * tq, tq)
        k = jnp.concatenate([k_ref[0, pl.ds(start, tq), :], kx_ref[0, 0, pl.ds(start, tq), :]], axis=1)
        for rows in row_groups:
            q = jnp.concatenate([q_ref[0, rows, :], qx_ref[0, 0, rows, :]], axis=1)
            s_sc[slot, rows, :] = lax.dot_general(q, k, contract_last, preferred_element_type=F32)

    def accumulate(kb, slot):
        start = pl.multiple_of(kb * tq, tq)
        v = jnp.concatenate([v_ref[0, pl.ds(start, tq), :], ones], axis=1)
        visible = (qi - kb) * tq
        for rows in row_groups:
            ahead = (lax.broadcasted_iota(jnp.int32, (rs, tq), 1)
                     - lax.broadcasted_iota(jnp.int32, (rs, tq), 0) - rows.start)
            s = jnp.where(ahead <= visible, s_sc[slot, rows, :], NEG_BIG)
            m_prev = m_sc[rows, :]
            m_new = jnp.maximum(m_prev, jnp.max(s, axis=1, keepdims=True))
            alpha = jnp.exp2(m_prev - m_new)
            p = jnp.exp2(s - jnp.tile(m_new, (1, nrep)))
            acc_sc[rows, :] = (jnp.tile(alpha, (1, 2)) * acc_sc[rows, :]
                               + jnp.dot(p.astype(BF16), v, preferred_element_type=F32))
            m_sc[rows, :] = m_new

    scores(0, 0)

    def pair(j, carry):
        kb = 2 * j
        scores(kb + 1, 1)
        accumulate(kb, 0)
        scores(jnp.minimum(kb + 2, qi), 0)
        accumulate(kb + 1, 1)
        return carry

    n_blocks = qi + 1
    lax.fori_loop(0, n_blocks // 2, pair, 0)

    @pl.when(n_blocks % 2 == 1)
    def _():
        accumulate(qi, 0)

    gate = g_ref[0].astype(F32)
    o_ref[0] = (acc_sc[:, :HEAD_DIM] / acc_sc[:, HEAD_DIM:] * _silu(gate)).astype(o_ref.dtype)


def _fox_attention(p3, qx, kx, n_heads, tq):
    b, l, _ = p3.shape
    nq = l // tq
    kern = functools.partial(_fox_kernel, tq=tq, rs=min(ROW_GROUP, tq))
    return pl.pallas_call(
        kern,
        grid=(b, n_heads, nq),
        in_specs=[pl.BlockSpec((1, tq, HEAD_DIM), lambda bi, h, qi: (bi, qi, h)),
                  pl.BlockSpec((1, 1, tq, HEAD_DIM), lambda bi, h, qi: (bi, h, qi, 0)),
                  pl.BlockSpec((1, l, HEAD_DIM), lambda bi, h, qi: (bi, 0, n_heads + h)),
                  pl.BlockSpec((1, 1, l, HEAD_DIM), lambda bi, h, qi: (bi, h, 0, 0)),
                  pl.BlockSpec((1, l, HEAD_DIM), lambda bi, h, qi: (bi, 0, 2 * n_heads + h)),
                  pl.BlockSpec((1, tq, HEAD_DIM), lambda bi, h, qi: (bi, qi, 3 * n_heads + h))],
        out_specs=pl.BlockSpec((1, tq, HEAD_DIM), lambda bi, h, qi: (bi, qi, h)),
        out_shape=jax.ShapeDtypeStruct((b, l, n_heads * HEAD_DIM), BF16),
        scratch_shapes=[pltpu.VMEM((2, tq, tq), F32), pltpu.VMEM((tq, HEAD_DIM), F32),
                        pltpu.VMEM((tq, 2 * HEAD_DIM), F32)],
        compiler_params=_params("parallel", "parallel", "arbitrary"), name="fox_attn",
    )(p3, qx, p3, kx, p3, p3)


def _sb_kernel(q_ref, k_ref, v_ref, g_ref, tri_ref, o_ref, cost_sc, acc_sc, *, tq, tk):
    qi = pl.program_id(2)
    ng = tq // tk
    cost_sc[...] = jnp.zeros_like(cost_sc)
    acc_sc[...] = jnp.zeros_like(acc_sc)

    def tile(r, kb, diagonal):
        rows = pl.ds(r * tk, tk)
        start = pl.multiple_of(kb * tk, tk)
        k = k_ref[0, pl.ds(start, tk), :]
        v = v_ref[0, pl.ds(start, tk), :]
        z = lax.dot_general(q_ref[0, rows, :], k, (((1,), (1,)), ((), ())), preferred_element_type=F32)
        softplus = jnp.maximum(z, 0.0) + jnp.log(1.0 + jnp.exp(-jnp.abs(z)))
        log_beta = z - softplus
        if diagonal:
            mask = (lax.broadcasted_iota(jnp.int32, (tk, tk), 1)
                    < lax.broadcasted_iota(jnp.int32, (tk, tk), 0))
            softplus = jnp.where(mask, softplus, 0.0)
        within = jnp.dot(softplus.astype(BF16), tri_ref[...], preferred_element_type=F32)
        w = jnp.exp(log_beta - within - jnp.tile(cost_sc[rows, :], (1, tk // HEAD_DIM)))
        if diagonal:
            w = jnp.where(mask, w, 0.0)
        acc_sc[rows, :] += jnp.dot(w.astype(BF16), v, preferred_element_type=F32)
        cost_sc[rows, :] += jnp.sum(softplus, axis=1, keepdims=True)

    for r in range(ng):
        tile(r, qi * ng + r, True)

    def live(state):
        return state[1] < SB_EXP_UNDERFLOW

    def body(state):
        for r in range(ng):
            tile(r, qi * ng + r - state[0], False)
        return state[0] + 1, jnp.min(cost_sc[...])

    state = lax.while_loop(lambda s: jnp.logical_and(s[0] <= qi * ng, live(s)), body,
                           (1, jnp.min(cost_sc[...])))

    def tail(state):
        for r in range(1, ng):
            kb = qi * ng + r - state[0]

            @pl.when(kb >= 0)
            def _():
                tile(r, kb, False)
        return state[0] + 1, jnp.min(cost_sc[...])

    lax.while_loop(lambda s: jnp.logical_and(s[0] <= qi * ng + ng - 1, live(s)), tail, state)
    gate = g_ref[0].astype(F32)
    o_ref[0] = (acc_sc[...] * _silu(gate)).astype(o_ref.dtype)


def _sb_attention(p3, col0, n_heads, tq, tk):
    b, l, _ = p3.shape
    nq = l // tq
    tri = jnp.tril(jnp.ones((tk, tk), F32), k=-1).astype(BF16)
    kern = functools.partial(_sb_kernel, tq=tq, tk=tk)
    return pl.pallas_call(
        kern,
        grid=(b, n_heads, nq),
        in_specs=[pl.BlockSpec((1, tq, HEAD_DIM), lambda bi, h, qi: (bi, qi, col0 + h)),
                  pl.BlockSpec((1, l, HEAD_DIM), lambda bi, h, qi: (bi, 0, col0 + n_heads + h)),
                  pl.BlockSpec((1, l, HEAD_DIM), lambda bi, h, qi: (bi, 0, col0 + 2 * n_heads + h)),
                  pl.BlockSpec((1, tq, HEAD_DIM), lambda bi, h, qi: (bi, qi, col0 + 3 * n_heads + h)),
                  pl.BlockSpec((tk, tk), lambda bi, h, qi: (0, 0))],
        out_specs=pl.BlockSpec((1, tq, HEAD_DIM), lambda bi, h, qi: (bi, qi, h)),
        out_shape=jax.ShapeDtypeStruct((b, l, n_heads * HEAD_DIM), BF16),
        scratch_shapes=[pltpu.VMEM((tq, HEAD_DIM), F32)] * 2,
        compiler_params=_params("parallel", "parallel", "arbitrary"), name="sb_attn",
    )(p3, p3, p3, p3, tri)


def _ssm_weights(a_re, a_im, log_dt, b_re, b_im, c_re, c_im):
    g, p = a_re.shape
    hg = b_re.shape[-1]
    c = SSM_CHUNK
    dt = jnp.exp(log_dt.astype(F32))[:, None]
    a_re = a_re.astype(F32)
    a_im = a_im.astype(F32)
    n = jnp.arange(c + 1, dtype=F32)[:, None, None]
    mag = jnp.exp(a_re * dt * n)
    ang = a_im * dt * n
    pw_re, pw_im = mag * jnp.cos(ang), mag * jnp.sin(ang)
    num_re, num_im = pw_re[1] - 1.0, pw_im[1]
    den = a_re * a_re + a_im * a_im
    f_re = (num_re * a_re + num_im * a_im) / den
    f_im = (num_im * a_re - num_re * a_im) / den
    bb_re = f_re[..., None] * b_re - f_im[..., None] * b_im
    bb_im = f_re[..., None] * b_im + f_im[..., None] * b_re
    cc_re, cc_im = c_re.astype(F32), c_im.astype(F32)
    hi = lax.Precision.HIGHEST
    cl_re = cc_re[None] * pw_re[:, :, None, :] - cc_im[None] * pw_im[:, :, None, :]
    cl_im = cc_re[None] * pw_im[:, :, None, :] + cc_im[None] * pw_re[:, :, None, :]
    kk = (jnp.einsum('ngop,gpi->ngoi', cl_re[:c], bb_re, precision=hi)
          - jnp.einsum('ngop,gpi->ngoi', cl_im[:c], bb_im, precision=hi))
    ug = SSM_UNIT // hg
    nu = g // ug
    eye = jnp.eye(ug, dtype=F32)
    bd = jnp.einsum('nugoi,gh->nugiho', kk.reshape(c, nu, ug, hg, hg), eye)
    bd = bd.reshape(c, nu, SSM_UNIT, SSM_UNIT).astype(BF16)
    zero = jnp.zeros_like(bd[0])
    m = jnp.concatenate([jnp.concatenate([bd[t - s] if t >= s else zero for t in range(c)], axis=2)
                         for s in range(c)], axis=1)
    n_rev = jnp.arange(c - 1, -1, -1, dtype=F32)[:, None, None]
    pr = jnp.exp(a_re * dt * n_rev) * jnp.cos(a_im * dt * n_rev)
    pi_ = jnp.exp(a_re * dt * n_rev) * jnp.sin(a_im * dt * n_rev)
    be_re = pr[..., None] * bb_re[None] - pi_[..., None] * bb_im[None]
    be_im = pr[..., None] * bb_im[None] + pi_[..., None] * bb_re[None]

    def bend(x):
        x = jnp.einsum('sugpi,gh->usgihp', x.reshape(c, nu, ug, p, hg), eye)
        return x.reshape(nu, c * SSM_UNIT, ug * p).astype(BF16)

    def cout(x):
        x = jnp.einsum('tugop,gh->tugpho', x.reshape(c, nu, ug, hg, p), eye)
        x = x.reshape(c, nu, ug * p, SSM_UNIT).astype(BF16)
        return jnp.concatenate([x[t] for t in range(c)], axis=2)

    lam_re = pw_re[c].reshape(nu, 1, ug * p)
    lam_im = pw_im[c].reshape(nu, 1, ug * p)
    return m, bend(be_re), bend(be_im), cout(cl_re[1:]), cout(-cl_im[1:]), lam_re, lam_im


def _ssm_kernel(x_ref, m_ref, bre_ref, bim_ref, cre_ref, cim_ref, lre_ref, lim_ref, y_ref,
                xf_sc, zre_sc, zim_sc, yf_sc, *, n_chunk):
    c = SSM_CHUNK
    xf_sc[...] = x_ref[0].astype(F32)
    u = jnp.concatenate([xf_sc[pl.ds(s, n_chunk, stride=c), :] for s in range(c)], axis=1).astype(BF16)
    zre_sc[...] = jnp.dot(u, bre_ref[0], preferred_element_type=F32)
    zim_sc[...] = jnp.dot(u, bim_ref[0], preferred_element_type=F32)
    lre = lre_ref[0]
    lim = lim_ref[0]

    def body(kc, carry):
        xr, xi = carry
        zr = zre_sc[pl.ds(kc, 1), :]
        zi = zim_sc[pl.ds(kc, 1), :]
        zre_sc[pl.ds(kc, 1), :] = xr
        zim_sc[pl.ds(kc, 1), :] = xi
        return lre * xr - lim * xi + zr, lre * xi + lim * xr + zi

    zero = jnp.zeros((1, zre_sc.shape[1]), F32)
    lax.fori_loop(0, n_chunk, body, (zero, zero))
    y = (jnp.dot(u, m_ref[0], preferred_element_type=F32)
         + jnp.dot(zre_sc[...].astype(BF16), cre_ref[0], preferred_element_type=F32)
         + jnp.dot(zim_sc[...].astype(BF16), cim_ref[0], preferred_element_type=F32))
    for t in range(c):
        yf_sc[pl.ds(t, n_chunk, stride=c), :] = y[:, t * SSM_UNIT:(t + 1) * SSM_UNIT]
    y_ref[0] = yf_sc[...].astype(y_ref.dtype)


def _ssm_scan(p3, x_blk0, weights, layer):
    b, l, _ = p3.shape
    m, bre, bim, cre, cim, lre, lim = weights
    nu = m.shape[1]
    lanes = m.shape[2]
    sp = bre.shape[3]
    nck = l // SSM_CHUNK
    kern = functools.partial(_ssm_kernel, n_chunk=nck)
    per_unit = lambda r, cc: pl.BlockSpec((None, 1, r, cc), lambda u, bi: (layer, u, 0, 0))
    return pl.pallas_call(
        kern,
        grid=(nu, b),
        in_specs=[pl.BlockSpec((1, l, SSM_UNIT), lambda u, bi: (bi, 0, x_blk0 + u)),
                  per_unit(lanes, lanes), per_unit(lanes, sp), per_unit(lanes, sp),
                  per_unit(sp, lanes), per_unit(sp, lanes), per_unit(1, sp), per_unit(1, sp)],
        out_specs=pl.BlockSpec((1, l, SSM_UNIT), lambda u, bi: (bi, 0, u)),
        out_shape=jax.ShapeDtypeStruct((b, l, nu * SSM_UNIT), BF16),
        scratch_shapes=[pltpu.VMEM((l, SSM_UNIT), F32), pltpu.VMEM((nck, sp), F32),
                        pltpu.VMEM((nck, sp), F32), pltpu.VMEM((l, SSM_UNIT), F32)],
        compiler_params=_params("parallel", "arbitrary"), name="ssm_scan",
    )(p3, m, bre, bim, cre, cim, lre, lim)


def _glu_kernel(y_ref, u_ref, g_ref, d_ref, w_ref, bg_ref, o_ref):
    y = y_ref[...].astype(F32) + d_ref[...] * u_ref[...].astype(F32)
    y = jax.nn.gelu(y)
    z = jnp.dot(y.astype(BF16), w_ref[...], preferred_element_type=F32) + bg_ref[...]
    o_ref[...] = (y * jax.nn.sigmoid(z) * _silu(g_ref[...].astype(F32))).astype(o_ref.dtype)


def _ssm_glu(y, p2, u_blk, gate_blk, d_skip, w_glu, b_glu, layer, tm):
    t, bw = y.shape
    return pl.pallas_call(
        _glu_kernel,
        grid=(t // tm,),
        in_specs=[pl.BlockSpec((tm, bw), lambda i: (i, 0)),
                  pl.BlockSpec((tm, bw), lambda i: (i, u_blk)),
                  pl.BlockSpec((tm, bw), lambda i: (i, gate_blk)),
                  pl.BlockSpec((1, bw), lambda i: (0, 0)),
                  pl.BlockSpec((None, bw, bw), lambda i: (layer, 0, 0)),
                  pl.BlockSpec((1, bw), lambda i: (0, 0))],
        out_specs=pl.BlockSpec((tm, bw), lambda i: (i, 0)),
        out_shape=jax.ShapeDtypeStruct((t, bw), BF16),
        compiler_params=_params("parallel"), name="ssm_glu",
    )(y, p2, p2, d_skip.reshape(1, bw).astype(F32), w_glu, b_glu.reshape(1, bw).astype(F32))


def _lru_kernel(x_ref, g_ref, cw_ref, cb_ref, wg_ref, br_ref, bi_ref, lam_ref, o_ref,
                tail_sc, h_sc, a_sc, b_sc, *, tl):
    @pl.when(pl.program_id(1) == 0)
    def _():
        tail_sc[...] = jnp.zeros_like(tail_sc)
        h_sc[...] = jnp.zeros_like(h_sc)

    x = x_ref[0].astype(F32)
    w = x.shape[1]
    xcat = jnp.concatenate([tail_sc[...], x], axis=0)
    xc = cb_ref[...] + cw_ref[CONV_WIDTH - 1:CONV_WIDTH, :] * x
    for d in range(1, CONV_WIDTH):
        xd = pltpu.roll(xcat, d, 0)[SUBLANES:, :]
        xc = xc + cw_ref[CONV_WIDTH - 1 - d:CONV_WIDTH - d, :] * xd
    tail_sc[...] = x[tl - SUBLANES:, :]

    xb = xc.astype(BF16)
    nblk = w // MXU_DIM
    gates = [jnp.dot(xb[:, kb * MXU_DIM:(kb + 1) * MXU_DIM], wg_ref[kb], preferred_element_type=F32)
             for kb in range(nblk)]
    gr = jnp.concatenate([gt[:, :MXU_DIM] for gt in gates], axis=1)
    gi = jnp.concatenate([gt[:, MXU_DIM:] for gt in gates], axis=1)
    r = jax.nn.sigmoid(gr + br_ref[...])
    i = jax.nn.sigmoid(gi + bi_ref[...])
    log_a = r * (LRU_C * _log_sigmoid(lam_ref[...]))
    a = jnp.exp(log_a)
    gap = 1.0 - a * a
    bx = gap * lax.rsqrt(jnp.maximum(gap, TINY)) * (i * xc)

    nt = tl // SUBLANES
    a = a.reshape(nt, SUBLANES, w)
    bx = bx.reshape(nt, SUBLANES, w)
    sub = lax.broadcasted_iota(jnp.int32, (nt, SUBLANES, w), 1)
    d = 1
    while d < SUBLANES:
        a_prev = pltpu.roll(a, d, 1)
        b_prev = pltpu.roll(bx, d, 1)
        use = sub >= d
        bx = jnp.where(use, a * b_prev + bx, bx)
        a = jnp.where(use, a * a_prev, a)
        d *= 2
    a_sc[...] = a.reshape(tl, w)
    b_sc[...] = bx.reshape(tl, w)

    def body(kt, h_last):
        s = pl.multiple_of(kt * SUBLANES, SUBLANES)
        ht = a_sc[pl.ds(s, SUBLANES), :] * h_last + b_sc[pl.ds(s, SUBLANES), :]
        b_sc[pl.ds(s, SUBLANES), :] = ht
        return ht[SUBLANES - 1:, :]

    h_sc[...] = lax.fori_loop(0, tl // SUBLANES, body, h_sc[...])
    o_ref[0] = (b_sc[...] * _silu(g_ref[0].astype(F32))).astype(o_ref.dtype)


def _lru_gate_tiles(w_r, w_i):
    depth, nb = w_r.shape[:2]
    per = MXU_DIM // LRU_BLOCK
    nblk = nb // per

    def tile_diag(wg):
        wg = wg.astype(F32).reshape(depth, nblk, per, LRU_BLOCK, LRU_BLOCK)
        eye = jnp.eye(per, dtype=F32)
        return jnp.einsum('dkaio,ab->dkaibo', wg, eye).reshape(depth, nblk, MXU_DIM, MXU_DIM)

    return jnp.concatenate([tile_diag(w_r), tile_diag(w_i)], axis=3).astype(BF16)


def _lru(p3, x_blk, gate_blk, conv_w, conv_b, wg, b_r, b_i, lam, layer, tl):
    b, l, _ = p3.shape
    bw = conv_w.shape[1]
    nblk = bw // MXU_DIM
    vec = lambda a: a.reshape(1, bw).astype(F32)
    kern = functools.partial(_lru_kernel, tl=tl)
    full = lambda shape: pl.BlockSpec(shape, lambda bi, ti: (0,) * len(shape))
    return pl.pallas_call(
        kern,
        grid=(b, l // tl),
        in_specs=[pl.BlockSpec((1, tl, bw), lambda bi, ti: (bi, ti, x_blk)),
                  pl.BlockSpec((1, tl, bw), lambda bi, ti: (bi, ti, gate_blk)),
                  full((CONV_WIDTH, bw)), full((1, bw)),
                  pl.BlockSpec((None, nblk, MXU_DIM, 2 * MXU_DIM), lambda bi, ti: (layer, 0, 0, 0)),
                  full((1, bw)), full((1, bw)), full((1, bw))],
        out_specs=pl.BlockSpec((1, tl, bw), lambda bi, ti: (bi, ti, 0)),
        out_shape=jax.ShapeDtypeStruct((b, l, bw), BF16),
        scratch_shapes=[pltpu.VMEM((SUBLANES, bw), F32), pltpu.VMEM((1, bw), F32),
                        pltpu.VMEM((tl, bw), F32), pltpu.VMEM((tl, bw), F32)],
        compiler_params=_params("parallel", "arbitrary"), name="rg_lru",
    )(p3, p3, conv_w.astype(F32), vec(conv_b), wg, vec(b_r), vec(b_i), vec(lam))


def _tiles(b, l):
    t = b * l
    return dict(
        norm=min(256, t),
        mm_m=min(1024, t),
        mm_n=1024,
        cum=min(512, l),
        fox=min(512, l),
        sb_q=min(512, l),
        sb_k=min(MXU_DIM, l),
        glu=min(512, t),
        lru=min(512, l),
    )


def kernel(x, g_pre, w_in, b_forget, ssm_a_re, ssm_a_im, ssm_log_dt, ssm_b_re, ssm_b_im, ssm_c_re, ssm_c_im, ssm_d, w_glu, b_glu, conv_w, conv_b, w_rgate, b_rgate, w_igate, b_igate, lru_lambda, w_out, g_post):
    b, l, d = x.shape
    depth = w_in.shape[0]
    bw = d // 4
    nh = bw // HEAD_DIM
    t = b * l
    ts = _tiles(b, l)
    assert w_in.shape[2] == 12 * bw + nh and l % SSM_CHUNK == 0

    def scaled_q(w, nblk, scale):
        sc = jnp.ones((nblk,), F32).at[0].set(scale)
        return (w.reshape(depth, d, nblk, bw) * sc[None, None, :, None]).reshape(depth, d, nblk * bw).astype(BF16)

    w_fox = scaled_q(w_in[:, :, :4 * bw], 4, HEAD_DIM ** -0.5 * LOG2E)
    w_rest = scaled_q(w_in[:, :, 4 * bw + nh:], 8, HEAD_DIM ** -0.5)
    w_f = jnp.pad(w_in[:, :, 4 * bw:4 * bw + nh], ((0, 0), (0, 0), (0, HEAD_DIM - nh))).astype(BF16)
    b_f = jnp.pad(b_forget.astype(F32), ((0, 0), (0, HEAD_DIM - nh))).reshape(depth, 1, HEAD_DIM)
    ssm_w = jax.vmap(_ssm_weights)(ssm_a_re, ssm_a_im, ssm_log_dt, ssm_b_re, ssm_b_im, ssm_c_re, ssm_c_im)
    lru_wg = _lru_gate_tiles(w_rgate, w_igate)
    w_glu_b = w_glu.astype(BF16)
    w_out_b = w_out.astype(BF16)

    h = x.reshape(t, d)
    u = _rmsnorm(h, g_pre[0], ts["norm"])
    for layer in range(depth):
        pf, log_f = _inproj(u, w_fox, w_f, b_f, layer, ts["mm_m"], ts["mm_n"])
        pr = _inproj_plain(u, w_rest, layer, ts["mm_m"], ts["mm_n"])
        pf3 = pf.reshape(b, l, 4 * bw)
        pr3 = pr.reshape(b, l, 8 * bw)

        qx, kx = _forget_columns(log_f.reshape(b, l, HEAD_DIM), nh, ts["cum"])
        o_fox = _fox_attention(pf3, qx, kx, nh, ts["fox"])
        o_sb = _sb_attention(pr3, 0, nh, ts["sb_q"], ts["sb_k"])

        y_ssm = _ssm_scan(pr3, 4 * bw // SSM_UNIT, ssm_w, layer)
        o_ssm = _ssm_glu(y_ssm.reshape(t, bw), pr, 4, 5, ssm_d[layer], w_glu_b, b_glu[layer], layer, ts["glu"])

        o_lru = _lru(pr3, 6, 7, conv_w[layer], conv_b[layer], lru_wg, b_rgate[layer], b_igate[layer],
                     lru_lambda[layer], layer, ts["lru"])

        y = _outproj([o_fox.reshape(t, bw), o_sb.reshape(t, bw), o_ssm, o_lru.reshape(t, bw)],
                     w_out_b, layer, ts["mm_m"], ts["mm_n"])
        g_next = g_pre[layer + 1] if layer + 1 < depth else None
        h, u = _post(y, h, g_post[layer], g_next, ts["norm"])
    return h.reshape(b, l, d)
```

```python
import functools
import math

import jax
import jax.numpy as jnp
from jax import lax
from jax.experimental import pallas as pl
from jax.experimental.pallas import tpu as pltpu

F32 = jnp.float32
BF16 = jnp.bfloat16

HEAD_DIM = 128
SSM_GROUP = 16
SSM_STATE = 64
SSM_UNIT = 128
SSM_CHUNK = 8
LRU_BLOCK = 64
LRU_C = 8.0
CONV_WIDTH = 4
RMS_EPS = 1e-6
NEG_BIG = -1e30
LOG2E = math.log2(math.e)
TINY = 1e-37
SB_EXP_UNDERFLOW = 110.0
ROW_GROUP = 256
SUBLANES = 8
LANES = 128
MXU_DIM = 256
VMEM_LIMIT = 56 * 1024 * 1024


def _params(*sem):
    return pltpu.CompilerParams(dimension_semantics=sem, vmem_limit_bytes=VMEM_LIMIT)


def _log_sigmoid(x):
    return jnp.minimum(x, 0.0) - jnp.log1p(jnp.exp(-jnp.abs(x)))


def _silu(x):
    return x * jax.nn.sigmoid(x)


def _rmsnorm_kernel(x_ref, g_ref, u_ref):
    x = x_ref[...]
    ms = jnp.mean(x * x, axis=-1, keepdims=True)
    u_ref[...] = (x * lax.rsqrt(ms + RMS_EPS) * g_ref[...]).astype(u_ref.dtype)


def _rmsnorm(x, g, tm):
    t, d = x.shape
    return pl.pallas_call(
        _rmsnorm_kernel,
        grid=(t // tm,),
        in_specs=[pl.BlockSpec((tm, d), lambda i: (i, 0)),
                  pl.BlockSpec((1, d), lambda i: (0, 0))],
        out_specs=pl.BlockSpec((tm, d), lambda i: (i, 0)),
        out_shape=jax.ShapeDtypeStruct((t, d), BF16),
        compiler_params=_params("parallel"), name="rmsnorm_first",
    )(x, g.reshape(1, d))


def _post_kernel(y_ref, h_ref, gpost_ref, gnext_ref, hn_ref, u_ref):
    y = y_ref[...].astype(F32)
    ms = jnp.mean(y * y, axis=-1, keepdims=True)
    hn = h_ref[...] + y * lax.rsqrt(ms + RMS_EPS) * gpost_ref[...]
    hn_ref[...] = hn
    ms2 = jnp.mean(hn * hn, axis=-1, keepdims=True)
    u_ref[...] = (hn * lax.rsqrt(ms2 + RMS_EPS) * gnext_ref[...]).astype(u_ref.dtype)


def _post_last_kernel(y_ref, h_ref, gpost_ref, hn_ref):
    y = y_ref[...].astype(F32)
    ms = jnp.mean(y * y, axis=-1, keepdims=True)
    hn_ref[...] = h_ref[...] + y * lax.rsqrt(ms + RMS_EPS) * gpost_ref[...]


def _post(y, h, g_post, g_next, tm):
    t, d = y.shape
    row = pl.BlockSpec((tm, d), lambda i: (i, 0))
    vec = pl.BlockSpec((1, d), lambda i: (0, 0))
    if g_next is None:
        return pl.pallas_call(
            _post_last_kernel, grid=(t // tm,),
            in_specs=[row, row, vec], out_specs=row,
            out_shape=jax.ShapeDtypeStruct((t, d), F32),
            compiler_params=_params("parallel"), name="post_norm_last",
        )(y, h, g_post.reshape(1, d)), None
    return pl.pallas_call(
        _post_kernel, grid=(t // tm,),
        in_specs=[row, row, vec, vec], out_specs=[row, row],
        out_shape=[jax.ShapeDtypeStruct((t, d), F32), jax.ShapeDtypeStruct((t, d), BF16)],
        compiler_params=_params("parallel"), name="post_norm",
    )(y, h, g_post.reshape(1, d), g_next.reshape(1, d))


def _inproj_kernel(u_ref, w_ref, wf_ref, bf_ref, p_ref, lf_ref):
    u = u_ref[...]
    p_ref[...] = jnp.dot(u, w_ref[...], preferred_element_type=F32).astype(p_ref.dtype)

    @pl.when(pl.program_id(1) == 0)
    def _():
        logit = jnp.dot(u, wf_ref[...], preferred_element_type=F32) + bf_ref[...]
        lf_ref[...] = _log_sigmoid(logit)


def _wcast_kernel(*refs, shift, scale):
    if shift:
        a_ref, b_ref, o_ref = refs
        x = jnp.concatenate([a_ref[:, shift:], b_ref[:, :shift]], axis=1)
    else:
        a_ref, o_ref = refs
        x = a_ref[...]
    s = jnp.where(pl.program_id(2) == 0, scale, 1.0)
    o_ref[...] = (x * s).astype(o_ref.dtype)


def _wcast(w, col0, ncols, scale, tr, tn):
    depth, d, _ = w.shape
    shift = col0 % LANES
    base = (col0 - shift) // tn
    assert (col0 - shift) % tn == 0 and ncols % tn == 0
    in_specs = [pl.BlockSpec((None, tr, tn), lambda l, i, j: (l, i, base + j))]
    args = [w]
    if shift:
        nxt = tn // LANES
        in_specs.append(pl.BlockSpec((None, tr, LANES), lambda l, i, j: (l, i, (base + j + 1) * nxt)))
        args.append(w)
    return pl.pallas_call(
        functools.partial(_wcast_kernel, shift=shift, scale=scale),
        grid=(depth, d // tr, ncols // tn),
        in_specs=in_specs,
        out_specs=pl.BlockSpec((None, tr, tn), lambda l, i, j: (l, i, j)),
        out_shape=jax.ShapeDtypeStruct((depth, d, ncols), BF16),
        compiler_params=_params("parallel", "parallel", "arbitrary"), name="weight_cast",
    )(*args)


def _matmul_kernel(u_ref, w_ref, p_ref):
    p_ref[...] = jnp.dot(u_ref[...], w_ref[...], preferred_element_type=F32).astype(p_ref.dtype)


def _inproj_plain(u, w, layer, tm, tn):
    t, d = u.shape
    n = w.shape[2]
    return pl.pallas_call(
        _matmul_kernel,
        grid=(t // tm, n // tn),
        in_specs=[pl.BlockSpec((tm, d), lambda i, j: (i, 0)),
                  pl.BlockSpec((None, d, tn), lambda i, j: (layer, 0, j))],
        out_specs=pl.BlockSpec((tm, tn), lambda i, j: (i, j)),
        out_shape=jax.ShapeDtypeStruct((t, n), BF16),
        compiler_params=_params("parallel", "arbitrary"), name="in_proj_rest",
    )(u, w)


def _inproj(u, w, wf, bf, layer, tm, tn):
    t, d = u.shape
    n = w.shape[2]
    return pl.pallas_call(
        _inproj_kernel,
        grid=(t // tm, n // tn),
        in_specs=[pl.BlockSpec((tm, d), lambda i, j: (i, 0)),
                  pl.BlockSpec((None, d, tn), lambda i, j: (layer, 0, j)),
                  pl.BlockSpec((None, d, HEAD_DIM), lambda i, j: (layer, 0, 0)),
                  pl.BlockSpec((None, 1, HEAD_DIM), lambda i, j: (layer, 0, 0))],
        out_specs=[pl.BlockSpec((tm, tn), lambda i, j: (i, j)),
                   pl.BlockSpec((tm, HEAD_DIM), lambda i, j: (i, 0))],
        out_shape=[jax.ShapeDtypeStruct((t, n), BF16),
                   jax.ShapeDtypeStruct((t, HEAD_DIM), F32)],
        compiler_params=_params("parallel", "arbitrary"), name="in_proj",
    )(u, w, wf, bf)


def _outproj_kernel(a_ref, b_ref, c_ref, d_ref, w_ref, y_ref):
    x = jnp.concatenate([a_ref[...], b_ref[...], c_ref[...], d_ref[...]], axis=1)
    y_ref[...] = jnp.dot(x, w_ref[...], preferred_element_type=F32).astype(y_ref.dtype)


def _outproj(branches, w, layer, tm, tn):
    t, bw = branches[0].shape
    _, k, n = w.shape
    bspec = pl.BlockSpec((tm, bw), lambda i, j: (i, 0))
    return pl.pallas_call(
        _outproj_kernel,
        grid=(t // tm, n // tn),
        in_specs=[bspec, bspec, bspec, bspec, pl.BlockSpec((None, k, tn), lambda i, j: (layer, 0, j))],
        out_specs=pl.BlockSpec((tm, tn), lambda i, j: (i, j)),
        out_shape=jax.ShapeDtypeStruct((t, n), BF16),
        compiler_params=_params("parallel", "arbitrary"), name="out_proj",
    )(*branches, w)


def _split3(x):
    hi = x.astype(BF16).astype(F32)
    r1 = x - hi
    mid = r1.astype(BF16).astype(F32)
    lo = (r1 - mid).astype(BF16).astype(F32)
    return hi, mid, lo


def _cumsum_kernel(lf_ref, tri_ref, qx_ref, kx_ref, carry_ref, *, n_heads):
    @pl.when(pl.program_id(1) == 0)
    def _():
        carry_ref[...] = jnp.zeros_like(carry_ref)

    hi, mid, lo = _split3(lf_ref[0])
    tri = tri_ref[...]
    c = (jnp.dot(tri, hi.astype(BF16), preferred_element_type=F32)
         + jnp.dot(tri, mid.astype(BF16), preferred_element_type=F32)
         + jnp.dot(tri, lo.astype(BF16), preferred_element_type=F32)) + carry_ref[...]
    carry_ref[...] = c[-1:, :]

    c2 = c * LOG2E
    lane = lax.broadcasted_iota(jnp.int32, c2.shape, 1)
    ones = jnp.where(lane < 3, 1.0, 0.0)
    for h in range(n_heads):
        f = jnp.sum(jnp.where(lane == h, c2, 0.0), axis=1, keepdims=True)
        hi, mid, lo = _split3(f)
        pieces = jnp.where(lane == 0, hi, jnp.where(lane == 1, mid, jnp.where(lane == 2, lo, 0.0)))
        qx_ref[0, h] = (pieces + pltpu.roll(ones, 3, 1)).astype(qx_ref.dtype)
        kx_ref[0, h] = (ones - pltpu.roll(pieces, 3, 1)).astype(kx_ref.dtype)


def _forget_columns(lf, n_heads, tc):
    b, l, w = lf.shape
    tri = jnp.tril(jnp.ones((tc, tc), F32)).astype(BF16)
    out = jax.ShapeDtypeStruct((b, n_heads, l, w), BF16)
    ospec = pl.BlockSpec((1, n_heads, tc, w), lambda i, j: (i, 0, j, 0))
    return pl.pallas_call(
        functools.partial(_cumsum_kernel, n_heads=n_heads),
        grid=(b, l // tc),
        in_specs=[pl.BlockSpec((1, tc, w), lambda i, j: (i, j, 0)),
                  pl.BlockSpec((tc, tc), lambda i, j: (0, 0))],
        out_specs=[ospec, ospec],
        out_shape=[out, out],
        scratch_shapes=[pltpu.VMEM((1, w), F32)],
        compiler_params=_params("parallel", "arbitrary"), name="forget_cumsum",
    )(lf, tri)


def _fox_kernel(q_ref, qx_ref, k_ref, kx_ref, v_ref, g_ref, o_ref, s_sc, m_sc, acc_sc, *, tq, rs):
    qi = pl.program_id(2)
    m_sc[...] = jnp.full_like(m_sc, NEG_BIG)
    acc_sc[...] = jnp.zeros_like(acc_sc)
    nrep = tq // HEAD_DIM
    ones = jnp.ones((tq, HEAD_DIM), BF16)
    contract_last = (((1,), (1,)), ((), ()))
    row_groups = [pl.ds(r0, rs) for r0 in range(0, tq, rs)]

    def scores(kb, slot):
        start = pl.multiple_of(kb * tq, tq)
        k = jnp.concatenate([k_ref[0, pl.ds(start, tq), :], kx_ref[0, 0, pl.ds(start, tq), :]], axis=1)
        for rows in row_groups:
            q = jnp.concatenate([q_ref[0, rows, :], qx_ref[0, 0, rows, :]], axis=1)
            s_sc[slot, rows, :] = lax.dot_general(q, k, contract_last, preferred_element_type=F32)

    def accumulate(kb, slot):
        start = pl.multiple_of(kb * tq, tq)
        v = jnp.concatenate([v_ref[0, pl.ds(start, tq), :], ones], axis=1)
        visible = (qi - kb) * tq
        for rows in row_groups:
            ahead = (lax.broadcasted_iota(jnp.int32, (rs, tq), 1)
                     - lax.broadcasted_iota(jnp.int32, (rs, tq), 0) - rows.start)
            s = jnp.where(ahead <= visible, s_sc[slot, rows, :], NEG_BIG)
            m_prev = m_sc[rows, :]
            m_new = jnp.maximum(m_prev, jnp.max(s, axis=1, keepdims=True))
            alpha = jnp.exp2(m_prev - m_new)
            p = jnp.exp2(s - jnp.tile(m_new, (1, nrep)))
            acc_sc[rows, :] = (jnp.tile(alpha, (1, 2)) * acc_sc[rows, :]
                               + jnp.dot(p.astype(BF16), v, preferred_element_type=F32))
            m_sc[rows, :] = m_new

    scores(0, 0)

    def pair(kb):
        scores(kb + 1, 1)
        accumulate(kb, 0)
        scores(jnp.minimum(kb + 2, qi), 0)
        accumulate(kb + 1, 1)

    def quad(j, carry):
        pair(4 * j)
        pair(4 * j + 2)
        return carry

    n_blocks = qi + 1
    n_quads = n_blocks // 4
    lax.fori_loop(0, n_quads, quad, 0)
    left = n_blocks - 4 * n_quads

    @pl.when(left >= 2)
    def _():
        pair(4 * n_quads)

    @pl.when(left % 2 == 1)
    def _():
        accumulate(qi, 0)

    gate = g_ref[0].astype(F32)
    o_ref[0] = (acc_sc[:, :HEAD_DIM] / acc_sc[:, HEAD_DIM:] * _silu(gate)).astype(o_ref.dtype)


def _fox_attention(p3, qx, kx, n_heads, tq):
    b, l, _ = p3.shape
    nq = l // tq
    kern = functools.partial(_fox_kernel, tq=tq, rs=min(ROW_GROUP, tq))
    return pl.pallas_call(
        kern,
        grid=(b, n_heads, nq),
        in_specs=[pl.BlockSpec((1, tq, HEAD_DIM), lambda bi, h, qi: (bi, qi, h)),
                  pl.BlockSpec((1, 1, tq, HEAD_DIM), lambda bi, h, qi: (bi, h, qi, 0)),
                  pl.BlockSpec((1, l, HEAD_DIM), lambda bi, h, qi: (bi, 0, n_heads + h)),
                  pl.BlockSpec((1, 1, l, HEAD_DIM), lambda bi, h, qi: (bi, h, 0, 0)),
                  pl.BlockSpec((1, l, HEAD_DIM), lambda bi, h, qi: (bi, 0, 2 * n_heads + h)),
                  pl.BlockSpec((1, tq, HEAD_DIM), lambda bi, h, qi: (bi, qi, 3 * n_heads + h))],
        out_specs=pl.BlockSpec((1, tq, HEAD_DIM), lambda bi, h, qi: (bi, qi, h)),
        out_shape=jax.ShapeDtypeStruct((b, l, n_heads * HEAD_DIM), BF16),
        scratch_shapes=[pltpu.VMEM((2, tq, tq), F32), pltpu.VMEM((tq, HEAD_DIM), F32),
                        pltpu.VMEM((tq, 2 * HEAD_DIM), F32)],
        compiler_params=_params("parallel", "parallel", "arbitrary"), name="fox_attn",
    )(p3, qx, p3, kx, p3, p3)


def _sb_kernel(q_ref, k_ref, v_ref, g_ref, tri_ref, o_ref, cost_sc, acc_sc, *, tq, tk):
    qi = pl.program_id(2)
    ng = tq // tk
    cost_sc[...] = jnp.zeros_like(cost_sc)
    acc_sc[...] = jnp.zeros_like(acc_sc)

    def tile(r, kb, diagonal):
        rows = pl.ds(r * tk, tk)
        start = pl.multiple_of(kb * tk, tk)
        k = k_ref[0, pl.ds(start, tk), :]
        v = v_ref[0, pl.ds(start, tk), :]
        z = lax.dot_general(q_ref[0, rows, :], k, (((1,), (1,)), ((), ())), preferred_element_type=F32)
        softplus = jnp.maximum(z, 0.0) + jnp.log(1.0 + jnp.exp(-jnp.abs(z)))
        log_beta = z - softplus
        if diagonal:
            mask = (lax.broadcasted_iota(jnp.int32, (tk, tk), 1)
                    < lax.broadcasted_iota(jnp.int32, (tk, tk), 0))
            softplus = jnp.where(mask, softplus, 0.0)
        within = jnp.dot(softplus.astype(BF16), tri_ref[...], preferred_element_type=F32)
        w = jnp.exp(log_beta - within - jnp.tile(cost_sc[rows, :], (1, tk // HEAD_DIM)))
        if diagonal:
            w = jnp.where(mask, w, 0.0)
        acc_sc[rows, :] += jnp.dot(w.astype(BF16), v, preferred_element_type=F32)
        cost_sc[rows, :] += jnp.sum(softplus, axis=1, keepdims=True)

    for r in range(ng):
        tile(r, qi * ng + r, True)

    def live(state):
        return state[1] < SB_EXP_UNDERFLOW

    def body(state):
        for r in range(ng):
            tile(r, qi * ng + r - state[0], False)
        return state[0] + 1, jnp.min(cost_sc[...])

    state = lax.while_loop(lambda s: jnp.logical_and(s[0] <= qi * ng, live(s)), body,
                           (1, jnp.min(cost_sc[...])))

    def tail(state):
        for r in range(1, ng):
            kb = qi * ng + r - state[0]

            @pl.when(kb >= 0)
            def _():
                tile(r, kb, False)
        return state[0] + 1, jnp.min(cost_sc[...])

    lax.while_loop(lambda s: jnp.logical_and(s[0] <= qi * ng + ng - 1, live(s)), tail, state)
    gate = g_ref[0].astype(F32)
    o_ref[0] = (acc_sc[...] * _silu(gate)).astype(o_ref.dtype)


def _sb_attention(p3, col0, n_heads, tq, tk):
    b, l, _ = p3.shape
    nq = l // tq
    tri = jnp.tril(jnp.ones((tk, tk), F32), k=-1).astype(BF16)
    kern = functools.partial(_sb_kernel, tq=tq, tk=tk)
    return pl.pallas_call(
        kern,
        grid=(b, n_heads, nq),
        in_specs=[pl.BlockSpec((1, tq, HEAD_DIM), lambda bi, h, qi: (bi, qi, col0 + h)),
                  pl.BlockSpec((1, l, HEAD_DIM), lambda bi, h, qi: (bi, 0, col0 + n_heads + h)),
                  pl.BlockSpec((1, l, HEAD_DIM), lambda bi, h, qi: (bi, 0, col0 + 2 * n_heads + h)),
                  pl.BlockSpec((1, tq, HEAD_DIM), lambda bi, h, qi: (bi, qi, col0 + 3 * n_heads + h)),
                  pl.BlockSpec((tk, tk), lambda bi, h, qi: (0, 0))],
        out_specs=pl.BlockSpec((1, tq, HEAD_DIM), lambda bi, h, qi: (bi, qi, h)),
        out_shape=jax.ShapeDtypeStruct((b, l, n_heads * HEAD_DIM), BF16),
        scratch_shapes=[pltpu.VMEM((tq, HEAD_DIM), F32)] * 2,
        compiler_params=_params("parallel", "parallel", "arbitrary"), name="sb_attn",
    )(p3, p3, p3, p3, tri)


def _ssm_weights(a_re, a_im, log_dt, b_re, b_im, c_re, c_im):
    g, p = a_re.shape
    hg = b_re.shape[-1]
    c = SSM_CHUNK
    dt = jnp.exp(log_dt.astype(F32))[:, None]
    a_re = a_re.astype(F32)
    a_im = a_im.astype(F32)
    n = jnp.arange(c + 1, dtype=F32)[:, None, None]
    mag = jnp.exp(a_re * dt * n)
    ang = a_im * dt * n
    pw_re, pw_im = mag * jnp.cos(ang), mag * jnp.sin(ang)
    num_re, num_im = pw_re[1] - 1.0, pw_im[1]
    den = a_re * a_re + a_im * a_im
    f_re = (num_re * a_re + num_im * a_im) / den
    f_im = (num_im * a_re - num_re * a_im) / den
    bb_re = f_re[..., None] * b_re - f_im[..., None] * b_im
    bb_im = f_re[..., None] * b_im + f_im[..., None] * b_re
    cc_re, cc_im = c_re.astype(F32), c_im.astype(F32)
    hi = lax.Precision.HIGHEST
    cl_re = cc_re[None] * pw_re[:, :, None, :] - cc_im[None] * pw_im[:, :, None, :]
    cl_im = cc_re[None] * pw_im[:, :, None, :] + cc_im[None] * pw_re[:, :, None, :]
    kk = (jnp.einsum('ngop,gpi->ngoi', cl_re[:c], bb_re, precision=hi)
          - jnp.einsum('ngop,gpi->ngoi', cl_im[:c], bb_im, precision=hi))
    ug = SSM_UNIT // hg
    nu = g // ug

    def spread(x, inner):
        rows, w = x.shape
        y = jnp.dot(x.astype(BF16), jnp.tile(jnp.eye(w, dtype=BF16), (1, ug)), preferred_element_type=F32)
        row_group = (lax.broadcasted_iota(jnp.int32, y.shape, 0) // inner) % ug
        col_group = lax.broadcasted_iota(jnp.int32, y.shape, 1) // w
        return jnp.where(row_group == col_group, y, 0.0).astype(BF16)

    bd = spread(kk.transpose(0, 1, 3, 2).reshape(c * g * hg, hg), hg).reshape(c, nu, SSM_UNIT, SSM_UNIT)
    zero = jnp.zeros_like(bd[0])
    m = jnp.concatenate([jnp.concatenate([bd[t - s] if t >= s else zero for t in range(c)], axis=2)
                         for s in range(c)], axis=1)
    n_rev = jnp.arange(c - 1, -1, -1, dtype=F32)[:, None, None]
    pr = jnp.exp(a_re * dt * n_rev) * jnp.cos(a_im * dt * n_rev)
    pi_ = jnp.exp(a_re * dt * n_rev) * jnp.sin(a_im * dt * n_rev)
    be_re = pr[..., None] * bb_re[None] - pi_[..., None] * bb_im[None]
    be_im = pr[..., None] * bb_im[None] + pi_[..., None] * bb_re[None]

    def bend(x):
        x = x.reshape(c, nu, ug, p, hg).transpose(1, 0, 2, 4, 3)
        return spread(x.reshape(nu * c * SSM_UNIT, p), hg).reshape(nu, c * SSM_UNIT, ug * p)

    def cout(x):
        x = x.reshape(c, nu, ug, hg, p).transpose(0, 1, 2, 4, 3)
        x = spread(x.reshape(c * nu * ug * p, hg), p).reshape(c, nu, ug * p, SSM_UNIT)
        return jnp.concatenate([x[t] for t in range(c)], axis=2)

    lam_re = pw_re[c].reshape(nu, 1, ug * p)
    lam_im = pw_im[c].reshape(nu, 1, ug * p)
    return m, bend(be_re), bend(be_im), cout(cl_re[1:]), cout(-cl_im[1:]), lam_re, lam_im


def _ssm_kernel(x_ref, m_ref, bre_ref, bim_ref, cre_ref, cim_ref, lre_ref, lim_ref, y_ref,
                xf_sc, zre_sc, zim_sc, yf_sc, *, n_chunk):
    c = SSM_CHUNK
    xf_sc[...] = x_ref[0].astype(F32)
    u = jnp.concatenate([xf_sc[pl.ds(s, n_chunk, stride=c), :] for s in range(c)], axis=1).astype(BF16)
    zre_sc[...] = jnp.dot(u, bre_ref[0], preferred_element_type=F32)
    zim_sc[...] = jnp.dot(u, bim_ref[0], preferred_element_type=F32)
    lre = lre_ref[0]
    lim = lim_ref[0]

    def body(kc, carry):
        xr, xi = carry
        zr = zre_sc[pl.ds(kc, 1), :]
        zi = zim_sc[pl.ds(kc, 1), :]
        zre_sc[pl.ds(kc, 1), :] = xr
        zim_sc[pl.ds(kc, 1), :] = xi
        return lre * xr - lim * xi + zr, lre * xi + lim * xr + zi

    zero = jnp.zeros((1, zre_sc.shape[1]), F32)
    lax.fori_loop(0, n_chunk, body, (zero, zero))
    y = (jnp.dot(u, m_ref[0], preferred_element_type=F32)
         + jnp.dot(zre_sc[...].astype(BF16), cre_ref[0], preferred_element_type=F32)
         + jnp.dot(zim_sc[...].astype(BF16), cim_ref[0], preferred_element_type=F32))
    for t in range(c):
        yf_sc[pl.ds(t, n_chunk, stride=c), :] = y[:, t * SSM_UNIT:(t + 1) * SSM_UNIT]
    y_ref[0] = yf_sc[...].astype(y_ref.dtype)


def _ssm_scan(p3, x_blk0, weights, layer):
    b, l, _ = p3.shape
    m, bre, bim, cre, cim, lre, lim = weights
    nu = m.shape[1]
    lanes = m.shape[2]
    sp = bre.shape[3]
    nck = l // SSM_CHUNK
    kern = functools.partial(_ssm_kernel, n_chunk=nck)
    per_unit = lambda r, cc: pl.BlockSpec((None, 1, r, cc), lambda u, bi: (layer, u, 0, 0))
    return pl.pallas_call(
        kern,
        grid=(nu, b),
        in_specs=[pl.BlockSpec((1, l, SSM_UNIT), lambda u, bi: (bi, 0, x_blk0 + u)),
                  per_unit(lanes, lanes), per_unit(lanes, sp), per_unit(lanes, sp),
                  per_unit(sp, lanes), per_unit(sp, lanes), per_unit(1, sp), per_unit(1, sp)],
        out_specs=pl.BlockSpec((1, l, SSM_UNIT), lambda u, bi: (bi, 0, u)),
        out_shape=jax.ShapeDtypeStruct((b, l, nu * SSM_UNIT), BF16),
        scratch_shapes=[pltpu.VMEM((l, SSM_UNIT), F32), pltpu.VMEM((nck, sp), F32),
                        pltpu.VMEM((nck, sp), F32), pltpu.VMEM((l, SSM_UNIT), F32)],
        compiler_params=_params("parallel", "arbitrary"), name="ssm_scan",
    )(p3, m, bre, bim, cre, cim, lre, lim)


def _glu_kernel(y_ref, u_ref, g_ref, d_ref, w_ref, bg_ref, o_ref):
    y = y_ref[...].astype(F32) + d_ref[...] * u_ref[...].astype(F32)
    y = jax.nn.gelu(y)
    z = jnp.dot(y.astype(BF16), w_ref[...], preferred_element_type=F32) + bg_ref[...]
    o_ref[...] = (y * jax.nn.sigmoid(z) * _silu(g_ref[...].astype(F32))).astype(o_ref.dtype)


def _ssm_glu(y, p2, u_blk, gate_blk, d_skip, w_glu, b_glu, layer, tm):
    t, bw = y.shape
    return pl.pallas_call(
        _glu_kernel,
        grid=(t // tm,),
        in_specs=[pl.BlockSpec((tm, bw), lambda i: (i, 0)),
                  pl.BlockSpec((tm, bw), lambda i: (i, u_blk)),
                  pl.BlockSpec((tm, bw), lambda i: (i, gate_blk)),
                  pl.BlockSpec((1, bw), lambda i: (0, 0)),
                  pl.BlockSpec((None, bw, bw), lambda i: (layer, 0, 0)),
                  pl.BlockSpec((1, bw), lambda i: (0, 0))],
        out_specs=pl.BlockSpec((tm, bw), lambda i: (i, 0)),
        out_shape=jax.ShapeDtypeStruct((t, bw), BF16),
        compiler_params=_params("parallel"), name="ssm_glu",
    )(y, p2, p2, d_skip.reshape(1, bw).astype(F32), w_glu, b_glu.reshape(1, bw).astype(F32))


def _lru_kernel(x_ref, g_ref, cw_ref, cb_ref, wg_ref, br_ref, bi_ref, lam_ref, o_ref,
                tail_sc, h_sc, a_sc, b_sc, *, tl):
    @pl.when(pl.program_id(1) == 0)
    def _():
        tail_sc[...] = jnp.zeros_like(tail_sc)
        h_sc[...] = jnp.zeros_like(h_sc)

    x = x_ref[0].astype(F32)
    w = x.shape[1]
    xcat = jnp.concatenate([tail_sc[...], x], axis=0)
    xc = cb_ref[...] + cw_ref[CONV_WIDTH - 1:CONV_WIDTH, :] * x
    for d in range(1, CONV_WIDTH):
        xd = pltpu.roll(xcat, d, 0)[SUBLANES:, :]
        xc = xc + cw_ref[CONV_WIDTH - 1 - d:CONV_WIDTH - d, :] * xd
    tail_sc[...] = x[tl - SUBLANES:, :]

    xb = xc.astype(BF16)
    nblk = w // MXU_DIM
    gates = [jnp.dot(xb[:, kb * MXU_DIM:(kb + 1) * MXU_DIM], wg_ref[kb], preferred_element_type=F32)
             for kb in range(nblk)]
    gr = jnp.concatenate([gt[:, :MXU_DIM] for gt in gates], axis=1)
    gi = jnp.concatenate([gt[:, MXU_DIM:] for gt in gates], axis=1)
    r = jax.nn.sigmoid(gr + br_ref[...])
    i = jax.nn.sigmoid(gi + bi_ref[...])
    log_a = r * (LRU_C * _log_sigmoid(lam_ref[...]))
    a = jnp.exp(log_a)
    gap = 1.0 - a * a
    bx = gap * lax.rsqrt(jnp.maximum(gap, TINY)) * (i * xc)

    nt = tl // SUBLANES
    a = a.reshape(nt, SUBLANES, w)
    bx = bx.reshape(nt, SUBLANES, w)
    sub = lax.broadcasted_iota(jnp.int32, (nt, SUBLANES, w), 1)
    d = 1
    while d < SUBLANES:
        a_prev = pltpu.roll(a, d, 1)
        b_prev = pltpu.roll(bx, d, 1)
        use = sub >= d
        bx = jnp.where(use, a * b_prev + bx, bx)
        a = jnp.where(use, a * a_prev, a)
        d *= 2
    a_sc[...] = a.reshape(tl, w)
    b_sc[...] = bx.reshape(tl, w)

    def body(kt, h_last):
        s = pl.multiple_of(kt * SUBLANES, SUBLANES)
        ht = a_sc[pl.ds(s, SUBLANES), :] * h_last + b_sc[pl.ds(s, SUBLANES), :]
        b_sc[pl.ds(s, SUBLANES), :] = ht
        return ht[SUBLANES - 1:, :]

    h_sc[...] = lax.fori_loop(0, tl // SUBLANES, body, h_sc[...])
    o_ref[0] = (b_sc[...] * _silu(g_ref[0].astype(F32))).astype(o_ref.dtype)


def _lru_gate_tiles(w_r, w_i):
    depth, nb = w_r.shape[:2]
    per = MXU_DIM // LRU_BLOCK
    nblk = nb // per

    def tile_diag(wg):
        wg = wg.astype(F32).reshape(depth, nblk, per, LRU_BLOCK, LRU_BLOCK)
        eye = jnp.eye(per, dtype=F32)
        return jnp.einsum('dkaio,ab->dkaibo', wg, eye).reshape(depth, nblk, MXU_DIM, MXU_DIM)

    return jnp.concatenate([tile_diag(w_r), tile_diag(w_i)], axis=3).astype(BF16)


def _lru(p3, x_blk, gate_blk, conv_w, conv_b, wg, b_r, b_i, lam, layer, tl):
    b, l, _ = p3.shape
    bw = conv_w.shape[1]
    nblk = bw // MXU_DIM
    vec = lambda a: a.reshape(1, bw).astype(F32)
    kern = functools.partial(_lru_kernel, tl=tl)
    full = lambda shape: pl.BlockSpec(shape, lambda bi, ti: (0,) * len(shape))
    return pl.pallas_call(
        kern,
        grid=(b, l // tl),
        in_specs=[pl.BlockSpec((1, tl, bw), lambda bi, ti: (bi, ti, x_blk)),
                  pl.BlockSpec((1, tl, bw), lambda bi, ti: (bi, ti, gate_blk)),
                  full((CONV_WIDTH, bw)), full((1, bw)),
                  pl.BlockSpec((None, nblk, MXU_DIM, 2 * MXU_DIM), lambda bi, ti: (layer, 0, 0, 0)),
                  full((1, bw)), full((1, bw)), full((1, bw))],
        out_specs=pl.BlockSpec((1, tl, bw), lambda bi, ti: (bi, ti, 0)),
        out_shape=jax.ShapeDtypeStruct((b, l, bw), BF16),
        scratch_shapes=[pltpu.VMEM((SUBLANES, bw), F32), pltpu.VMEM((1, bw), F32),
                        pltpu.VMEM((tl, bw), F32), pltpu.VMEM((tl, bw), F32)],
        compiler_params=_params("parallel", "arbitrary"), name="rg_lru",
    )(p3, p3, conv_w.astype(F32), vec(conv_b), wg, vec(b_r), vec(b_i), vec(lam))


def _tiles(b, l):
    t = b * l
    return dict(
        norm=min(256, t),
        mm_m=min(1024, t),
        mm_n=1024,
        cum=min(512, l),
        fox=min(512, l),
        wcast=512,
        sb_q=min(1024, l),
        sb_k=min(MXU_DIM, l),
        glu=min(512, t),
        lru=min(512, l),
    )


def kernel(x, g_pre, w_in, b_forget, ssm_a_re, ssm_a_im, ssm_log_dt, ssm_b_re, ssm_b_im, ssm_c_re, ssm_c_im, ssm_d, w_glu, b_glu, conv_w, conv_b, w_rgate, b_rgate, w_igate, b_igate, lru_lambda, w_out, g_post):
    b, l, d = x.shape
    depth = w_in.shape[0]
    bw = d // 4
    nh = bw // HEAD_DIM
    t = b * l
    ts = _tiles(b, l)
    assert w_in.shape[2] == 12 * bw + nh and l % SSM_CHUNK == 0

    w_fox = _wcast(w_in, 0, 4 * bw, HEAD_DIM ** -0.5 * LOG2E, ts["wcast"], bw)
    w_rest = _wcast(w_in, 4 * bw + nh, 8 * bw, HEAD_DIM ** -0.5, ts["wcast"], bw)
    w_f = jnp.pad(w_in[:, :, 4 * bw:4 * bw + nh], ((0, 0), (0, 0), (0, HEAD_DIM - nh))).astype(BF16)
    b_f = jnp.pad(b_forget.astype(F32), ((0, 0), (0, HEAD_DIM - nh))).reshape(depth, 1, HEAD_DIM)
    ssm_w = jax.vmap(_ssm_weights)(ssm_a_re, ssm_a_im, ssm_log_dt, ssm_b_re, ssm_b_im, ssm_c_re, ssm_c_im)
    lru_wg = _lru_gate_tiles(w_rgate, w_igate)
    w_glu_b = w_glu.astype(BF16)
    w_out_b = w_out.astype(BF16)

    h = x.reshape(t, d)
    u = _rmsnorm(h, g_pre[0], ts["norm"])
    for layer in range(depth):
        pf, log_f = _inproj(u, w_fox, w_f, b_f, layer, ts["mm_m"], ts["mm_n"])
        pr = _inproj_plain(u, w_rest, layer, ts["mm_m"], ts["mm_n"])
        pf3 = pf.reshape(b, l, 4 * bw)
        pr3 = pr.reshape(b, l, 8 * bw)

        qx, kx = _forget_columns(log_f.reshape(b, l, HEAD_DIM), nh, ts["cum"])
        o_fox = _fox_attention(pf3, qx, kx, nh, ts["fox"])
        o_sb = _sb_attention(pr3, 0, nh, ts["sb_q"], ts["sb_k"])

        y_ssm = _ssm_scan(pr3, 4 * bw // SSM_UNIT, ssm_w, layer)
        o_ssm = _ssm_glu(y_ssm.reshape(t, bw), pr, 4, 5, ssm_d[layer], w_glu_b, b_glu[layer], layer, ts["glu"])

        o_lru = _lru(pr3, 6, 7, conv_w[layer], conv_b[layer], lru_wg, b_rgate[layer], b_igate[layer],
                     lru_lambda[layer], layer, ts["lru"])

        y = _outproj([o_fox.reshape(t, bw), o_sb.reshape(t, bw), o_ssm, o_lru.reshape(t, bw)],
                     w_out_b, layer, ts["mm_m"], ts["mm_n"])
        g_next = g_pre[layer + 1] if layer + 1 < depth else None
        h, u = _post(y, h, g_post[layer], g_next, ts["norm"])
    return h.reshape(b, l, d)
```

```python
import functools
import math

import jax
import jax.numpy as jnp
from jax import lax
from jax.experimental import pallas as pl
from jax.experimental.pallas import tpu as pltpu

F32 = jnp.float32
BF16 = jnp.bfloat16

HEAD_DIM = 128
SSM_GROUP = 16
SSM_STATE = 64
SSM_UNIT = 128
SSM_CHUNK = 8
LRU_BLOCK = 64
LRU_C = 8.0
CONV_WIDTH = 4
RMS_EPS = 1e-6
NEG_BIG = -1e30
LOG2E = math.log2(math.e)
TINY = 1e-37
SB_EXP_UNDERFLOW = 110.0
ROW_GROUP = 512
SUBLANES = 8
LANES = 128
MXU_DIM = 256
VMEM_LIMIT = 56 * 1024 * 1024


def _params(*sem):
    return pltpu.CompilerParams(dimension_semantics=sem, vmem_limit_bytes=VMEM_LIMIT)


def _log_sigmoid(x):
    return jnp.minimum(x, 0.0) - jnp.log1p(jnp.exp(-jnp.abs(x)))


def _silu(x):
    return x * jax.nn.sigmoid(x)


def _rmsnorm_kernel(x_ref, g_ref, u_ref):
    x = x_ref[...]
    ms = jnp.mean(x * x, axis=-1, keepdims=True)
    u_ref[...] = (x * lax.rsqrt(ms + RMS_EPS) * g_ref[...]).astype(u_ref.dtype)


def _rmsnorm(x, g, tm):
    t, d = x.shape
    return pl.pallas_call(
        _rmsnorm_kernel,
        grid=(t // tm,),
        in_specs=[pl.BlockSpec((tm, d), lambda i: (i, 0)),
                  pl.BlockSpec((1, d), lambda i: (0, 0))],
        out_specs=pl.BlockSpec((tm, d), lambda i: (i, 0)),
        out_shape=jax.ShapeDtypeStruct((t, d), BF16),
        compiler_params=_params("parallel"), name="rmsnorm_first",
    )(x, g.reshape(1, d))


def _post_kernel(y_ref, h_ref, gpost_ref, gnext_ref, hn_ref, u_ref):
    y = y_ref[...].astype(F32)
    ms = jnp.mean(y * y, axis=-1, keepdims=True)
    hn = h_ref[...] + y * lax.rsqrt(ms + RMS_EPS) * gpost_ref[...]
    hn_ref[...] = hn
    ms2 = jnp.mean(hn * hn, axis=-1, keepdims=True)
    u_ref[...] = (hn * lax.rsqrt(ms2 + RMS_EPS) * gnext_ref[...]).astype(u_ref.dtype)


def _post_last_kernel(y_ref, h_ref, gpost_ref, hn_ref):
    y = y_ref[...].astype(F32)
    ms = jnp.mean(y * y, axis=-1, keepdims=True)
    hn_ref[...] = h_ref[...] + y * lax.rsqrt(ms + RMS_EPS) * gpost_ref[...]


def _post(y, h, g_post, g_next, tm):
    t, d = y.shape
    row = pl.BlockSpec((tm, d), lambda i: (i, 0))
    vec = pl.BlockSpec((1, d), lambda i: (0, 0))
    if g_next is None:
        return pl.pallas_call(
            _post_last_kernel, grid=(t // tm,),
            in_specs=[row, row, vec], out_specs=row,
            out_shape=jax.ShapeDtypeStruct((t, d), F32),
            compiler_params=_params("parallel"), name="post_norm_last",
        )(y, h, g_post.reshape(1, d)), None
    return pl.pallas_call(
        _post_kernel, grid=(t // tm,),
        in_specs=[row, row, vec, vec], out_specs=[row, row],
        out_shape=[jax.ShapeDtypeStruct((t, d), F32), jax.ShapeDtypeStruct((t, d), BF16)],
        compiler_params=_params("parallel"), name="post_norm",
    )(y, h, g_post.reshape(1, d), g_next.reshape(1, d))


def _inproj_kernel(u_ref, w_ref, wf_ref, bf_ref, p_ref, lf_ref):
    u = u_ref[...]
    p_ref[...] = jnp.dot(u, w_ref[...], preferred_element_type=F32).astype(p_ref.dtype)

    @pl.when(pl.program_id(1) == 0)
    def _():
        logit = jnp.dot(u, wf_ref[...], preferred_element_type=F32) + bf_ref[...]
        lf_ref[...] = _log_sigmoid(logit)


def _wshift_kernel(a_ref, b_ref, o_ref, *, shift):
    x = jnp.concatenate([a_ref[:, shift:].astype(F32), b_ref[:, :shift].astype(F32)], axis=1)
    o_ref[...] = x.astype(o_ref.dtype)


def _wshift(w, col0, ncols, tr, tn):
    depth, d, _ = w.shape
    shift = col0 % LANES
    base = (col0 - shift) // tn
    assert shift and (col0 - shift) % tn == 0 and ncols % tn == 0
    nxt = tn // LANES
    return pl.pallas_call(
        functools.partial(_wshift_kernel, shift=shift),
        grid=(depth, d // tr, ncols // tn),
        in_specs=[pl.BlockSpec((None, tr, tn), lambda l, i, j: (l, i, base + j)),
                  pl.BlockSpec((None, tr, LANES), lambda l, i, j: (l, i, (base + j + 1) * nxt))],
        out_specs=pl.BlockSpec((None, tr, tn), lambda l, i, j: (l, i, j)),
        out_shape=jax.ShapeDtypeStruct((depth, d, ncols), w.dtype),
        compiler_params=_params("parallel", "parallel", "arbitrary"), name="weight_shift",
    )(w, w)


def _matmul_kernel(u_ref, w_ref, p_ref):
    p_ref[...] = jnp.dot(u_ref[...], w_ref[...], preferred_element_type=F32).astype(p_ref.dtype)


def _inproj_plain(u, w, layer, tm, tn):
    t, d = u.shape
    n = w.shape[2]
    return pl.pallas_call(
        _matmul_kernel,
        grid=(t // tm, n // tn),
        in_specs=[pl.BlockSpec((tm, d), lambda i, j: (i, 0)),
                  pl.BlockSpec((None, d, tn), lambda i, j: (layer, 0, j))],
        out_specs=pl.BlockSpec((tm, tn), lambda i, j: (i, j)),
        out_shape=jax.ShapeDtypeStruct((t, n), BF16),
        compiler_params=_params("parallel", "arbitrary"), name="in_proj_rest",
    )(u, w)


def _inproj(u, w, n, wf, bf, layer, tm, tn):
    t, d = u.shape
    return pl.pallas_call(
        _inproj_kernel,
        grid=(t // tm, n // tn),
        in_specs=[pl.BlockSpec((tm, d), lambda i, j: (i, 0)),
                  pl.BlockSpec((None, d, tn), lambda i, j: (layer, 0, j)),
                  pl.BlockSpec((None, d, HEAD_DIM), lambda i, j: (layer, 0, 0)),
                  pl.BlockSpec((None, 1, HEAD_DIM), lambda i, j: (layer, 0, 0))],
        out_specs=[pl.BlockSpec((tm, tn), lambda i, j: (i, j)),
                   pl.BlockSpec((tm, HEAD_DIM), lambda i, j: (i, 0))],
        out_shape=[jax.ShapeDtypeStruct((t, n), BF16),
                   jax.ShapeDtypeStruct((t, HEAD_DIM), F32)],
        compiler_params=_params("parallel", "arbitrary"), name="in_proj",
    )(u, w, wf, bf)


def _outproj_kernel(a_ref, b_ref, c_ref, d_ref, w_ref, y_ref):
    x = jnp.concatenate([a_ref[...], b_ref[...], c_ref[...], d_ref[...]], axis=1)
    y_ref[...] = jnp.dot(x, w_ref[...], preferred_element_type=F32).astype(y_ref.dtype)


def _outproj(branches, w, layer, tm, tn):
    t, bw = branches[0].shape
    _, k, n = w.shape
    bspec = pl.BlockSpec((tm, bw), lambda i, j: (i, 0))
    return pl.pallas_call(
        _outproj_kernel,
        grid=(t // tm, n // tn),
        in_specs=[bspec, bspec, bspec, bspec, pl.BlockSpec((None, k, tn), lambda i, j: (layer, 0, j))],
        out_specs=pl.BlockSpec((tm, tn), lambda i, j: (i, j)),
        out_shape=jax.ShapeDtypeStruct((t, n), BF16),
        compiler_params=_params("parallel", "arbitrary"), name="out_proj",
    )(*branches, w)


def _split3(x):
    hi = x.astype(BF16).astype(F32)
    r1 = x - hi
    mid = r1.astype(BF16).astype(F32)
    lo = (r1 - mid).astype(BF16).astype(F32)
    return hi, mid, lo


def _cumsum_kernel(lf_ref, tri_ref, qx_ref, kx_ref, carry_ref, *, n_heads):
    @pl.when(pl.program_id(1) == 0)
    def _():
        carry_ref[...] = jnp.zeros_like(carry_ref)

    hi, mid, lo = _split3(lf_ref[0])
    tri = tri_ref[...]
    c = (jnp.dot(tri, hi.astype(BF16), preferred_element_type=F32)
         + jnp.dot(tri, mid.astype(BF16), preferred_element_type=F32)
         + jnp.dot(tri, lo.astype(BF16), preferred_element_type=F32)) + carry_ref[...]
    carry_ref[...] = c[-1:, :]

    c2 = c * LOG2E
    lane = lax.broadcasted_iota(jnp.int32, c2.shape, 1)
    ones = jnp.where(lane < 3, 1.0, 0.0)
    for h in range(n_heads):
        f = jnp.sum(jnp.where(lane == h, c2, 0.0), axis=1, keepdims=True)
        hi, mid, lo = _split3(f)
        pieces = jnp.where(lane == 0, hi, jnp.where(lane == 1, mid, jnp.where(lane == 2, lo, 0.0)))
        qx_ref[0, h] = (pieces + pltpu.roll(ones, 3, 1)).astype(qx_ref.dtype)
        kx_ref[0, h] = (ones - pltpu.roll(pieces, 3, 1)).astype(kx_ref.dtype)


def _forget_columns(lf, n_heads, tc):
    b, l, w = lf.shape
    tri = jnp.tril(jnp.ones((tc, tc), F32)).astype(BF16)
    out = jax.ShapeDtypeStruct((b, n_heads, l, w), BF16)
    ospec = pl.BlockSpec((1, n_heads, tc, w), lambda i, j: (i, 0, j, 0))
    return pl.pallas_call(
        functools.partial(_cumsum_kernel, n_heads=n_heads),
        grid=(b, l // tc),
        in_specs=[pl.BlockSpec((1, tc, w), lambda i, j: (i, j, 0)),
                  pl.BlockSpec((tc, tc), lambda i, j: (0, 0))],
        out_specs=[ospec, ospec],
        out_shape=[out, out],
        scratch_shapes=[pltpu.VMEM((1, w), F32)],
        compiler_params=_params("parallel", "arbitrary"), name="forget_cumsum",
    )(lf, tri)


def _fox_kernel(q_ref, qx_ref, k_ref, kx_ref, v_ref, g_ref, o_ref, s_sc, m_sc, acc_sc, *, tq, rs):
    qi = pl.program_id(2)
    m_sc[...] = jnp.full_like(m_sc, NEG_BIG)
    acc_sc[...] = jnp.zeros_like(acc_sc)
    nrep = tq // HEAD_DIM
    ones = jnp.ones((tq, HEAD_DIM), BF16)
    contract_last = (((1,), (1,)), ((), ()))
    row_groups = [pl.ds(r0, rs) for r0 in range(0, tq, rs)]

    def scores(kb, slot):
        start = pl.multiple_of(kb * tq, tq)
        k = jnp.concatenate([k_ref[0, pl.ds(start, tq), :], kx_ref[0, 0, pl.ds(start, tq), :]], axis=1)
        for rows in row_groups:
            q = jnp.concatenate([q_ref[0, rows, :], qx_ref[0, 0, rows, :]], axis=1)
            s_sc[slot, rows, :] = lax.dot_general(q, k, contract_last, preferred_element_type=F32)

    def accumulate(kb, slot):
        start = pl.multiple_of(kb * tq, tq)
        v = jnp.concatenate([v_ref[0, pl.ds(start, tq), :], ones], axis=1)
        visible = (qi - kb) * tq
        for rows in row_groups:
            ahead = (lax.broadcasted_iota(jnp.int32, (rs, tq), 1)
                     - lax.broadcasted_iota(jnp.int32, (rs, tq), 0) - rows.start)
            s = jnp.where(ahead <= visible, s_sc[slot, rows, :], NEG_BIG)
            m_prev = m_sc[rows, :]
            m_new = jnp.maximum(m_prev, jnp.max(s, axis=1, keepdims=True))
            alpha = jnp.exp2(m_prev - m_new)
            p = jnp.exp2(s - jnp.tile(m_new, (1, nrep)))
            acc_sc[rows, :] = (jnp.tile(alpha, (1, 2)) * acc_sc[rows, :]
                               + jnp.dot(p.astype(BF16), v, preferred_element_type=F32))
            m_sc[rows, :] = m_new

    scores(0, 0)

    def pair(kb):
        scores(kb + 1, 1)
        accumulate(kb, 0)
        scores(jnp.minimum(kb + 2, qi), 0)
        accumulate(kb + 1, 1)

    def quad(j, carry):
        pair(4 * j)
        pair(4 * j + 2)
        return carry

    n_blocks = qi + 1
    n_quads = n_blocks // 4
    lax.fori_loop(0, n_quads, quad, 0)
    left = n_blocks - 4 * n_quads

    @pl.when(left >= 2)
    def _():
        pair(4 * n_quads)

    @pl.when(left % 2 == 1)
    def _():
        accumulate(qi, 0)

    gate = g_ref[0].astype(F32)
    o_ref[0] = (acc_sc[:, :HEAD_DIM] / acc_sc[:, HEAD_DIM:] * _silu(gate)).astype(o_ref.dtype)


def _fox_attention(p3, qx, kx, n_heads, tq):
    b, l, _ = p3.shape
    nq = l // tq
    kern = functools.partial(_fox_kernel, tq=tq, rs=min(ROW_GROUP, tq))
    return pl.pallas_call(
        kern,
        grid=(b, n_heads, nq),
        in_specs=[pl.BlockSpec((1, tq, HEAD_DIM), lambda bi, h, qi: (bi, qi, h)),
                  pl.BlockSpec((1, 1, tq, HEAD_DIM), lambda bi, h, qi: (bi, h, qi, 0)),
                  pl.BlockSpec((1, l, HEAD_DIM), lambda bi, h, qi: (bi, 0, n_heads + h)),
                  pl.BlockSpec((1, 1, l, HEAD_DIM), lambda bi, h, qi: (bi, h, 0, 0)),
                  pl.BlockSpec((1, l, HEAD_DIM), lambda bi, h, qi: (bi, 0, 2 * n_heads + h)),
                  pl.BlockSpec((1, tq, HEAD_DIM), lambda bi, h, qi: (bi, qi, 3 * n_heads + h))],
        out_specs=pl.BlockSpec((1, tq, HEAD_DIM), lambda bi, h, qi: (bi, qi, h)),
        out_shape=jax.ShapeDtypeStruct((b, l, n_heads * HEAD_DIM), BF16),
        scratch_shapes=[pltpu.VMEM((2, tq, tq), F32), pltpu.VMEM((tq, HEAD_DIM), F32),
                        pltpu.VMEM((tq, 2 * HEAD_DIM), F32)],
        compiler_params=_params("parallel", "parallel", "arbitrary"), name="fox_attn",
    )(p3, qx, p3, kx, p3, p3)


def _sb_kernel(q_ref, k_ref, v_ref, g_ref, tri_ref, o_ref, cost_sc, acc_sc, *, tq, tk):
    qi = pl.program_id(2)
    ng = tq // tk
    cost_sc[...] = jnp.zeros_like(cost_sc)
    acc_sc[...] = jnp.zeros_like(acc_sc)

    def tile(r, kb, diagonal):
        rows = pl.ds(r * tk, tk)
        start = pl.multiple_of(kb * tk, tk)
        k = k_ref[0, pl.ds(start, tk), :]
        v = v_ref[0, pl.ds(start, tk), :]
        z = lax.dot_general(q_ref[0, rows, :], k, (((1,), (1,)), ((), ())), preferred_element_type=F32)
        softplus = jnp.maximum(z, 0.0) + jnp.log(1.0 + jnp.exp(-jnp.abs(z)))
        log_beta = z - softplus
        if diagonal:
            mask = (lax.broadcasted_iota(jnp.int32, (tk, tk), 1)
                    < lax.broadcasted_iota(jnp.int32, (tk, tk), 0))
            softplus = jnp.where(mask, softplus, 0.0)
        within = jnp.dot(softplus.astype(BF16), tri_ref[...], preferred_element_type=F32)
        w = jnp.exp(log_beta - within - jnp.tile(cost_sc[rows, :], (1, tk // HEAD_DIM)))
        if diagonal:
            w = jnp.where(mask, w, 0.0)
        acc_sc[rows, :] += jnp.dot(w.astype(BF16), v, preferred_element_type=F32)
        cost_sc[rows, :] += jnp.sum(softplus, axis=1, keepdims=True)

    for r in range(ng):
        tile(r, qi * ng + r, True)

    def live(state):
        return state[1] < SB_EXP_UNDERFLOW

    def body(state):
        for r in range(ng):
            tile(r, qi * ng + r - state[0], False)
        return state[0] + 1, jnp.min(cost_sc[...])

    state = lax.while_loop(lambda s: jnp.logical_and(s[0] <= qi * ng, live(s)), body,
                           (1, jnp.min(cost_sc[...])))

    def tail(state):
        for r in range(1, ng):
            kb = qi * ng + r - state[0]

            @pl.when(kb >= 0)
            def _():
                tile(r, kb, False)
        return state[0] + 1, jnp.min(cost_sc[...])

    lax.while_loop(lambda s: jnp.logical_and(s[0] <= qi * ng + ng - 1, live(s)), tail, state)
    gate = g_ref[0].astype(F32)
    o_ref[0] = (acc_sc[...] * _silu(gate)).astype(o_ref.dtype)


def _sb_attention(p3, col0, n_heads, tq, tk):
    b, l, _ = p3.shape
    nq = l // tq
    tri = jnp.tril(jnp.ones((tk, tk), F32), k=-1).astype(BF16)
    kern = functools.partial(_sb_kernel, tq=tq, tk=tk)
    return pl.pallas_call(
        kern,
        grid=(b, n_heads, nq),
        in_specs=[pl.BlockSpec((1, tq, HEAD_DIM), lambda bi, h, qi: (bi, qi, col0 + h)),
                  pl.BlockSpec((1, l, HEAD_DIM), lambda bi, h, qi: (bi, 0, col0 + n_heads + h)),
                  pl.BlockSpec((1, l, HEAD_DIM), lambda bi, h, qi: (bi, 0, col0 + 2 * n_heads + h)),
                  pl.BlockSpec((1, tq, HEAD_DIM), lambda bi, h, qi: (bi, qi, col0 + 3 * n_heads + h)),
                  pl.BlockSpec((tk, tk), lambda bi, h, qi: (0, 0))],
        out_specs=pl.BlockSpec((1, tq, HEAD_DIM), lambda bi, h, qi: (bi, qi, h)),
        out_shape=jax.ShapeDtypeStruct((b, l, n_heads * HEAD_DIM), BF16),
        scratch_shapes=[pltpu.VMEM((tq, HEAD_DIM), F32)] * 2,
        compiler_params=_params("parallel", "parallel", "arbitrary"), name="sb_attn",
    )(p3, p3, p3, p3, tri)


def _ssm_weights(a_re, a_im, log_dt, b_re, b_im, c_re, c_im):
    g, p = a_re.shape
    hg = b_re.shape[-1]
    c = SSM_CHUNK
    dt = jnp.exp(log_dt.astype(F32))[:, None]
    a_re = a_re.astype(F32)
    a_im = a_im.astype(F32)
    n = jnp.arange(c + 1, dtype=F32)[:, None, None]
    mag = jnp.exp(a_re * dt * n)
    ang = a_im * dt * n
    pw_re, pw_im = mag * jnp.cos(ang), mag * jnp.sin(ang)
    num_re, num_im = pw_re[1] - 1.0, pw_im[1]
    den = a_re * a_re + a_im * a_im
    f_re = (num_re * a_re + num_im * a_im) / den
    f_im = (num_im * a_re - num_re * a_im) / den
    bb_re = f_re[..., None] * b_re - f_im[..., None] * b_im
    bb_im = f_re[..., None] * b_im + f_im[..., None] * b_re
    cc_re, cc_im = c_re.astype(F32), c_im.astype(F32)
    hi = lax.Precision.HIGHEST
    cl_re = cc_re[None] * pw_re[:, :, None, :] - cc_im[None] * pw_im[:, :, None, :]
    cl_im = cc_re[None] * pw_im[:, :, None, :] + cc_im[None] * pw_re[:, :, None, :]
    kk = jnp.einsum('ngop,gpi->ngoi', jnp.concatenate([cl_re[:c], cl_im[:c]], axis=3),
                    jnp.concatenate([bb_re, -bb_im], axis=1), precision=hi)
    ug = SSM_UNIT // hg
    nu = g // ug

    def spread(x, inner):
        rows, w = x.shape
        y = jnp.dot(x.astype(BF16), jnp.tile(jnp.eye(w, dtype=BF16), (1, ug)), preferred_element_type=F32)
        row_group = (lax.broadcasted_iota(jnp.int32, y.shape, 0) // inner) % ug
        col_group = lax.broadcasted_iota(jnp.int32, y.shape, 1) // w
        return jnp.where(row_group == col_group, y, 0.0).astype(BF16)

    bd = spread(kk.transpose(0, 1, 3, 2).reshape(c * g * hg, hg), hg).reshape(c, nu, SSM_UNIT, SSM_UNIT)
    zero = jnp.zeros_like(bd[0])
    m = jnp.concatenate([jnp.concatenate([bd[t - s] if t >= s else zero for t in range(c)], axis=2)
                         for s in range(c)], axis=1)
    n_rev = jnp.arange(c - 1, -1, -1, dtype=F32)[:, None, None]
    pr = jnp.exp(a_re * dt * n_rev) * jnp.cos(a_im * dt * n_rev)
    pi_ = jnp.exp(a_re * dt * n_rev) * jnp.sin(a_im * dt * n_rev)
    be_re = pr[..., None] * bb_re[None] - pi_[..., None] * bb_im[None]
    be_im = pr[..., None] * bb_im[None] + pi_[..., None] * bb_re[None]

    def bend(x):
        x = x.reshape(c, nu, ug, p, hg).transpose(1, 0, 2, 4, 3)
        return spread(x.reshape(nu * c * SSM_UNIT, p), hg).reshape(nu, c * SSM_UNIT, ug * p)

    def cout(x):
        x = x.reshape(c, nu, ug, hg, p).transpose(0, 1, 2, 4, 3)
        x = spread(x.reshape(c * nu * ug * p, hg), p).reshape(c, nu, ug * p, SSM_UNIT)
        return jnp.concatenate([x[t] for t in range(c)], axis=2)

    lam_re = pw_re[c].reshape(nu, 1, ug * p)
    lam_im = pw_im[c].reshape(nu, 1, ug * p)
    return m, bend(be_re), bend(be_im), cout(cl_re[1:]), cout(-cl_im[1:]), lam_re, lam_im


def _ssm_kernel(x_ref, m_ref, bre_ref, bim_ref, cre_ref, cim_ref, lre_ref, lim_ref, y_ref,
                xf_sc, zre_sc, zim_sc, yf_sc, *, n_chunk):
    c = SSM_CHUNK
    xf_sc[...] = x_ref[0].astype(F32)
    u = jnp.concatenate([xf_sc[pl.ds(s, n_chunk, stride=c), :] for s in range(c)], axis=1).astype(BF16)
    zre_sc[...] = jnp.dot(u, bre_ref[0], preferred_element_type=F32)
    zim_sc[...] = jnp.dot(u, bim_ref[0], preferred_element_type=F32)
    lre = lre_ref[0]
    lim = lim_ref[0]

    def body(kc, carry):
        xr, xi = carry
        zr = zre_sc[pl.ds(kc, 1), :]
        zi = zim_sc[pl.ds(kc, 1), :]
        zre_sc[pl.ds(kc, 1), :] = xr
        zim_sc[pl.ds(kc, 1), :] = xi
        return lre * xr - lim * xi + zr, lre * xi + lim * xr + zi

    zero = jnp.zeros((1, zre_sc.shape[1]), F32)
    lax.fori_loop(0, n_chunk, body, (zero, zero))
    y = (jnp.dot(u, m_ref[0], preferred_element_type=F32)
         + jnp.dot(zre_sc[...].astype(BF16), cre_ref[0], preferred_element_type=F32)
         + jnp.dot(zim_sc[...].astype(BF16), cim_ref[0], preferred_element_type=F32))
    for t in range(c):
        yf_sc[pl.ds(t, n_chunk, stride=c), :] = y[:, t * SSM_UNIT:(t + 1) * SSM_UNIT]
    y_ref[0] = yf_sc[...].astype(y_ref.dtype)


def _ssm_scan(p3, x_blk0, weights, layer):
    b, l, _ = p3.shape
    m, bre, bim, cre, cim, lre, lim = weights
    nu = m.shape[1]
    lanes = m.shape[2]
    sp = bre.shape[3]
    nck = l // SSM_CHUNK
    kern = functools.partial(_ssm_kernel, n_chunk=nck)
    per_unit = lambda r, cc: pl.BlockSpec((None, 1, r, cc), lambda u, bi: (layer, u, 0, 0))
    return pl.pallas_call(
        kern,
        grid=(nu, b),
        in_specs=[pl.BlockSpec((1, l, SSM_UNIT), lambda u, bi: (bi, 0, x_blk0 + u)),
                  per_unit(lanes, lanes), per_unit(lanes, sp), per_unit(lanes, sp),
                  per_unit(sp, lanes), per_unit(sp, lanes), per_unit(1, sp), per_unit(1, sp)],
        out_specs=pl.BlockSpec((1, l, SSM_UNIT), lambda u, bi: (bi, 0, u)),
        out_shape=jax.ShapeDtypeStruct((b, l, nu * SSM_UNIT), BF16),
        scratch_shapes=[pltpu.VMEM((l, SSM_UNIT), F32), pltpu.VMEM((nck, sp), F32),
                        pltpu.VMEM((nck, sp), F32), pltpu.VMEM((l, SSM_UNIT), F32)],
        compiler_params=_params("parallel", "arbitrary"), name="ssm_scan",
    )(p3, m, bre, bim, cre, cim, lre, lim)


def _glu_kernel(y_ref, u_ref, g_ref, d_ref, w_ref, bg_ref, o_ref):
    y = y_ref[...].astype(F32) + d_ref[...] * u_ref[...].astype(F32)
    y = jax.nn.gelu(y)
    z = jnp.dot(y.astype(BF16), w_ref[...], preferred_element_type=F32) + bg_ref[...]
    o_ref[...] = (y * jax.nn.sigmoid(z) * _silu(g_ref[...].astype(F32))).astype(o_ref.dtype)


def _ssm_glu(y, p2, u_blk, gate_blk, d_skip, w_glu, b_glu, layer, tm):
    t, bw = y.shape
    return pl.pallas_call(
        _glu_kernel,
        grid=(t // tm,),
        in_specs=[pl.BlockSpec((tm, bw), lambda i: (i, 0)),
                  pl.BlockSpec((tm, bw), lambda i: (i, u_blk)),
                  pl.BlockSpec((tm, bw), lambda i: (i, gate_blk)),
                  pl.BlockSpec((1, bw), lambda i: (0, 0)),
                  pl.BlockSpec((None, bw, bw), lambda i: (layer, 0, 0)),
                  pl.BlockSpec((1, bw), lambda i: (0, 0))],
        out_specs=pl.BlockSpec((tm, bw), lambda i: (i, 0)),
        out_shape=jax.ShapeDtypeStruct((t, bw), BF16),
        compiler_params=_params("parallel"), name="ssm_glu",
    )(y, p2, p2, d_skip.reshape(1, bw).astype(F32), w_glu, b_glu.reshape(1, bw).astype(F32))


def _lru_kernel(x_ref, g_ref, cw_ref, cb_ref, wg_ref, br_ref, bi_ref, lam_ref, o_ref,
                tail_sc, h_sc, a_sc, b_sc, *, tl):
    @pl.when(pl.program_id(1) == 0)
    def _():
        tail_sc[...] = jnp.zeros_like(tail_sc)
        h_sc[...] = jnp.zeros_like(h_sc)

    x = x_ref[0].astype(F32)
    w = x.shape[1]
    xcat = jnp.concatenate([tail_sc[...], x], axis=0)
    xc = cb_ref[...] + cw_ref[CONV_WIDTH - 1:CONV_WIDTH, :] * x
    for d in range(1, CONV_WIDTH):
        xd = pltpu.roll(xcat, d, 0)[SUBLANES:, :]
        xc = xc + cw_ref[CONV_WIDTH - 1 - d:CONV_WIDTH - d, :] * xd
    tail_sc[...] = x[tl - SUBLANES:, :]

    xb = xc.astype(BF16)
    nblk = w // MXU_DIM
    gates = [jnp.dot(xb[:, kb * MXU_DIM:(kb + 1) * MXU_DIM], wg_ref[kb], preferred_element_type=F32)
             for kb in range(nblk)]
    gr = jnp.concatenate([gt[:, :MXU_DIM] for gt in gates], axis=1)
    gi = jnp.concatenate([gt[:, MXU_DIM:] for gt in gates], axis=1)
    r = jax.nn.sigmoid(gr + br_ref[...])
    i = jax.nn.sigmoid(gi + bi_ref[...])
    log_a = r * (LRU_C * _log_sigmoid(lam_ref[...]))
    a = jnp.exp(log_a)
    gap = 1.0 - a * a
    bx = gap * lax.rsqrt(jnp.maximum(gap, TINY)) * (i * xc)

    nt = tl // SUBLANES
    a = a.reshape(nt, SUBLANES, w)
    bx = bx.reshape(nt, SUBLANES, w)
    sub = lax.broadcasted_iota(jnp.int32, (nt, SUBLANES, w), 1)
    d = 1
    while d < SUBLANES:
        a_prev = pltpu.roll(a, d, 1)
        b_prev = pltpu.roll(bx, d, 1)
        use = sub >= d
        bx = jnp.where(use, a * b_prev + bx, bx)
        a = jnp.where(use, a * a_prev, a)
        d *= 2
    a_sc[...] = a.reshape(tl, w)
    b_sc[...] = bx.reshape(tl, w)

    def body(kt, h_last):
        s = pl.multiple_of(kt * SUBLANES, SUBLANES)
        ht = a_sc[pl.ds(s, SUBLANES), :] * h_last + b_sc[pl.ds(s, SUBLANES), :]
        b_sc[pl.ds(s, SUBLANES), :] = ht
        return ht[SUBLANES - 1:, :]

    h_sc[...] = lax.fori_loop(0, tl // SUBLANES, body, h_sc[...])
    o_ref[0] = (b_sc[...] * _silu(g_ref[0].astype(F32))).astype(o_ref.dtype)


def _lru_gate_tiles(w_r, w_i):
    depth, nb = w_r.shape[:2]
    per = MXU_DIM // LRU_BLOCK
    nblk = nb // per

    def tile_diag(wg):
        wg = wg.astype(F32).reshape(depth, nblk, per, LRU_BLOCK, LRU_BLOCK)
        eye = jnp.eye(per, dtype=F32)
        return jnp.einsum('dkaio,ab->dkaibo', wg, eye).reshape(depth, nblk, MXU_DIM, MXU_DIM)

    return jnp.concatenate([tile_diag(w_r), tile_diag(w_i)], axis=3).astype(BF16)


def _lru(p3, x_blk, gate_blk, conv_w, conv_b, wg, b_r, b_i, lam, layer, tl):
    b, l, _ = p3.shape
    bw = conv_w.shape[1]
    nblk = bw // MXU_DIM
    vec = lambda a: a.reshape(1, bw).astype(F32)
    kern = functools.partial(_lru_kernel, tl=tl)
    full = lambda shape: pl.BlockSpec(shape, lambda bi, ti: (0,) * len(shape))
    return pl.pallas_call(
        kern,
        grid=(b, l // tl),
        in_specs=[pl.BlockSpec((1, tl, bw), lambda bi, ti: (bi, ti, x_blk)),
                  pl.BlockSpec((1, tl, bw), lambda bi, ti: (bi, ti, gate_blk)),
                  full((CONV_WIDTH, bw)), full((1, bw)),
                  pl.BlockSpec((None, nblk, MXU_DIM, 2 * MXU_DIM), lambda bi, ti: (layer, 0, 0, 0)),
                  full((1, bw)), full((1, bw)), full((1, bw))],
        out_specs=pl.BlockSpec((1, tl, bw), lambda bi, ti: (bi, ti, 0)),
        out_shape=jax.ShapeDtypeStruct((b, l, bw), BF16),
        scratch_shapes=[pltpu.VMEM((SUBLANES, bw), F32), pltpu.VMEM((1, bw), F32),
                        pltpu.VMEM((tl, bw), F32), pltpu.VMEM((tl, bw), F32)],
        compiler_params=_params("parallel", "arbitrary"), name="rg_lru",
    )(p3, p3, conv_w.astype(F32), vec(conv_b), wg, vec(b_r), vec(b_i), vec(lam))


def _tiles(b, l):
    t = b * l
    return dict(
        norm=min(256, t),
        mm_m=min(1024, t),
        mm_n=1024,
        cum=min(512, l),
        fox=min(512, l),
        wcast=512,
        sb_q=min(1024, l),
        sb_k=min(MXU_DIM, l),
        glu=min(512, t),
        lru=min(512, l),
    )


def kernel(x, g_pre, w_in, b_forget, ssm_a_re, ssm_a_im, ssm_log_dt, ssm_b_re, ssm_b_im, ssm_c_re, ssm_c_im, ssm_d, w_glu, b_glu, conv_w, conv_b, w_rgate, b_rgate, w_igate, b_igate, lru_lambda, w_out, g_post):
    b, l, d = x.shape
    depth = w_in.shape[0]
    bw = d // 4
    nh = bw // HEAD_DIM
    t = b * l
    ts = _tiles(b, l)
    assert w_in.shape[2] == 12 * bw + nh and l % SSM_CHUNK == 0

    col_scale = jnp.ones((w_in.shape[2],), F32)
    col_scale = col_scale.at[:bw].set(HEAD_DIM ** -0.5 * LOG2E)
    col_scale = col_scale.at[4 * bw + nh:5 * bw + nh].set(HEAD_DIM ** -0.5)
    w_all = (w_in * col_scale).astype(BF16)
    w_rest = _wshift(w_all, 4 * bw + nh, 8 * bw, ts["wcast"], bw)
    w_f = jnp.pad(w_all[:, :, 4 * bw:4 * bw + nh], ((0, 0), (0, 0), (0, HEAD_DIM - nh)))
    b_f = jnp.pad(b_forget.astype(F32), ((0, 0), (0, HEAD_DIM - nh))).reshape(depth, 1, HEAD_DIM)
    ssm_w = jax.vmap(_ssm_weights)(ssm_a_re, ssm_a_im, ssm_log_dt, ssm_b_re, ssm_b_im, ssm_c_re, ssm_c_im)
    lru_wg = _lru_gate_tiles(w_rgate, w_igate)
    w_glu_b = w_glu.astype(BF16)
    w_out_b = w_out.astype(BF16)

    h = x.reshape(t, d)
    u = _rmsnorm(h, g_pre[0], ts["norm"])
    for layer in range(depth):
        pf, log_f = _inproj(u, w_all, 4 * bw, w_f, b_f, layer, ts["mm_m"], ts["mm_n"])
        pr = _inproj_plain(u, w_rest, layer, ts["mm_m"], ts["mm_n"])
        pf3 = pf.reshape(b, l, 4 * bw)
        pr3 = pr.reshape(b, l, 8 * bw)

        qx, kx = _forget_columns(log_f.reshape(b, l, HEAD_DIM), nh, ts["cum"])
        o_fox = _fox_attention(pf3, qx, kx, nh, ts["fox"])
        o_sb = _sb_attention(pr3, 0, nh, ts["sb_q"], ts["sb_k"])

        y_ssm = _ssm_scan(pr3, 4 * bw // SSM_UNIT, ssm_w, layer)
        o_ssm = _ssm_glu(y_ssm.reshape(t, bw), pr, 4, 5, ssm_d[layer], w_glu_b, b_glu[layer], layer, ts["glu"])

        o_lru = _lru(pr3, 6, 7, conv_w[layer], conv_b[layer], lru_wg, b_rgate[layer], b_igate[layer],
                     lru_lambda[layer], layer, ts["lru"])

        y = _outproj([o_fox.reshape(t, bw), o_sb.reshape(t, bw), o_ssm, o_lru.reshape(t, bw)],
                     w_out_b, layer, ts["mm_m"], ts["mm_n"])
        g_next = g_pre[layer + 1] if layer + 1 < depth else None
        h, u = _post(y, h, g_post[layer], g_next, ts["norm"])
    return h.reshape(b, l, d)
```

```python
import functools
import math

import jax
import jax.numpy as jnp
from jax import lax
from jax.experimental import pallas as pl
from jax.experimental.pallas import tpu as pltpu

F32 = jnp.float32
BF16 = jnp.bfloat16

HEAD_DIM = 128
SSM_GROUP = 16
SSM_STATE = 64
SSM_UNIT = 128
SSM_CHUNK = 8
LRU_BLOCK = 64
LRU_C = 8.0
CONV_WIDTH = 4
RMS_EPS = 1e-6
NEG_BIG = -1e30
LOG2E = math.log2(math.e)
TINY = 1e-37
SB_EXP_UNDERFLOW = 110.0
SUBLANES = 8
MXU_DIM = 256
VMEM_LIMIT = 56 * 1024 * 1024


def _params(*sem):
    return pltpu.CompilerParams(dimension_semantics=sem, vmem_limit_bytes=VMEM_LIMIT)


def _log_sigmoid(x):
    return jnp.minimum(x, 0.0) - jnp.log1p(jnp.exp(-jnp.abs(x)))


def _silu(x):
    return x * jax.nn.sigmoid(x)


def _rmsnorm_kernel(x_ref, g_ref, u_ref):
    x = x_ref[...]
    ms = jnp.mean(x * x, axis=-1, keepdims=True)
    u_ref[...] = (x * lax.rsqrt(ms + RMS_EPS) * g_ref[...]).astype(u_ref.dtype)


def _rmsnorm(x, g, tm):
    t, d = x.shape
    return pl.pallas_call(
        _rmsnorm_kernel,
        grid=(t // tm,),
        in_specs=[pl.BlockSpec((tm, d), lambda i: (i, 0)),
                  pl.BlockSpec((1, d), lambda i: (0, 0))],
        out_specs=pl.BlockSpec((tm, d), lambda i: (i, 0)),
        out_shape=jax.ShapeDtypeStruct((t, d), BF16),
        compiler_params=_params("parallel"), name="rmsnorm_first",
    )(x, g.reshape(1, d))


def _post_kernel(y_ref, h_ref, gpost_ref, gnext_ref, hn_ref, u_ref):
    y = y_ref[...].astype(F32)
    ms = jnp.mean(y * y, axis=-1, keepdims=True)
    hn = h_ref[...] + y * lax.rsqrt(ms + RMS_EPS) * gpost_ref[...]
    hn_ref[...] = hn
    ms2 = jnp.mean(hn * hn, axis=-1, keepdims=True)
    u_ref[...] = (hn * lax.rsqrt(ms2 + RMS_EPS) * gnext_ref[...]).astype(u_ref.dtype)


def _post_last_kernel(y_ref, h_ref, gpost_ref, hn_ref):
    y = y_ref[...].astype(F32)
    ms = jnp.mean(y * y, axis=-1, keepdims=True)
    hn_ref[...] = h_ref[...] + y * lax.rsqrt(ms + RMS_EPS) * gpost_ref[...]


def _post(y, h, g_post, g_next, tm):
    t, d = y.shape
    row = pl.BlockSpec((tm, d), lambda i: (i, 0))
    vec = pl.BlockSpec((1, d), lambda i: (0, 0))
    if g_next is None:
        return pl.pallas_call(
            _post_last_kernel, grid=(t // tm,),
            in_specs=[row, row, vec], out_specs=row,
            out_shape=jax.ShapeDtypeStruct((t, d), F32),
            compiler_params=_params("parallel"), name="post_norm_last",
        )(y, h, g_post.reshape(1, d)), None
    return pl.pallas_call(
        _post_kernel, grid=(t // tm,),
        in_specs=[row, row, vec, vec], out_specs=[row, row],
        out_shape=[jax.ShapeDtypeStruct((t, d), F32), jax.ShapeDtypeStruct((t, d), BF16)],
        compiler_params=_params("parallel"), name="post_norm",
    )(y, h, g_post.reshape(1, d), g_next.reshape(1, d))


def _inproj_kernel(u_ref, w_ref, wf_ref, bf_ref, p_ref, lf_ref):
    u = u_ref[...]
    p_ref[...] = jnp.dot(u, w_ref[...], preferred_element_type=F32).astype(p_ref.dtype)

    @pl.when(pl.program_id(1) == 0)
    def _():
        logit = jnp.dot(u, wf_ref[...], preferred_element_type=F32) + bf_ref[...]
        lf_ref[...] = _log_sigmoid(logit)


def _matmul_kernel(u_ref, w_ref, p_ref):
    p_ref[...] = jnp.dot(u_ref[...], w_ref[...], preferred_element_type=F32).astype(p_ref.dtype)


def _inproj_plain(u, w, layer, tm, tn):
    t, d = u.shape
    n = w.shape[2]
    return pl.pallas_call(
        _matmul_kernel,
        grid=(t // tm, n // tn),
        in_specs=[pl.BlockSpec((tm, d), lambda i, j: (i, 0)),
                  pl.BlockSpec((None, d, tn), lambda i, j: (layer, 0, j))],
        out_specs=pl.BlockSpec((tm, tn), lambda i, j: (i, j)),
        out_shape=jax.ShapeDtypeStruct((t, n), BF16),
        compiler_params=_params("parallel", "arbitrary"), name="in_proj_rest",
    )(u, w)


def _inproj(u, w, n, wf, bf, layer, tm, tn):
    t, d = u.shape
    return pl.pallas_call(
        _inproj_kernel,
        grid=(t // tm, n // tn),
        in_specs=[pl.BlockSpec((tm, d), lambda i, j: (i, 0)),
                  pl.BlockSpec((None, d, tn), lambda i, j: (layer, 0, j)),
                  pl.BlockSpec((None, d, HEAD_DIM), lambda i, j: (layer, 0, 0)),
                  pl.BlockSpec((None, 1, HEAD_DIM), lambda i, j: (layer, 0, 0))],
        out_specs=[pl.BlockSpec((tm, tn), lambda i, j: (i, j)),
                   pl.BlockSpec((tm, HEAD_DIM), lambda i, j: (i, 0))],
        out_shape=[jax.ShapeDtypeStruct((t, n), BF16),
                   jax.ShapeDtypeStruct((t, HEAD_DIM), F32)],
        compiler_params=_params("parallel", "arbitrary"), name="in_proj",
    )(u, w, wf, bf)


def _outproj_kernel(a_ref, b_ref, c_ref, d_ref, w_ref, y_ref):
    x = jnp.concatenate([a_ref[...], b_ref[...], c_ref[...], d_ref[...]], axis=1)
    y_ref[...] = jnp.dot(x, w_ref[...], preferred_element_type=F32).astype(y_ref.dtype)


def _outproj(branches, w, layer, tm, tn):
    t, bw = branches[0].shape
    _, k, n = w.shape
    bspec = pl.BlockSpec((tm, bw), lambda i, j: (i, 0))
    return pl.pallas_call(
        _outproj_kernel,
        grid=(t // tm, n // tn),
        in_specs=[bspec, bspec, bspec, bspec, pl.BlockSpec((None, k, tn), lambda i, j: (layer, 0, j))],
        out_specs=pl.BlockSpec((tm, tn), lambda i, j: (i, j)),
        out_shape=jax.ShapeDtypeStruct((t, n), BF16),
        compiler_params=_params("parallel", "arbitrary"), name="out_proj",
    )(*branches, w)


def _split3(x):
    hi = x.astype(BF16).astype(F32)
    r1 = x - hi
    mid = r1.astype(BF16).astype(F32)
    lo = (r1 - mid).astype(BF16).astype(F32)
    return hi, mid, lo


def _cumsum_kernel(lf_ref, tri_ref, qx_ref, kx_ref, carry_ref, *, n_heads):
    @pl.when(pl.program_id(1) == 0)
    def _():
        carry_ref[...] = jnp.zeros_like(carry_ref)

    hi, mid, lo = _split3(lf_ref[0])
    tri = tri_ref[...]
    c = (jnp.dot(tri, hi.astype(BF16), preferred_element_type=F32)
         + jnp.dot(tri, mid.astype(BF16), preferred_element_type=F32)
         + jnp.dot(tri, lo.astype(BF16), preferred_element_type=F32)) + carry_ref[...]
    carry_ref[...] = c[-1:, :]

    c2 = c * LOG2E
    lane = lax.broadcasted_iota(jnp.int32, c2.shape, 1)
    ones = jnp.where(lane < 3, 1.0, 0.0)
    for h in range(n_heads):
        f = jnp.sum(jnp.where(lane == h, c2, 0.0), axis=1, keepdims=True)
        hi, mid, lo = _split3(f)
        pieces = jnp.where(lane == 0, hi, jnp.where(lane == 1, mid, jnp.where(lane == 2, lo, 0.0)))
        qx_ref[0, h] = (pieces + pltpu.roll(ones, 3, 1)).astype(qx_ref.dtype)
        kx_ref[0, h] = (ones - pltpu.roll(pieces, 3, 1)).astype(kx_ref.dtype)


def _forget_columns(lf, n_heads, tc):
    b, l, w = lf.shape
    tri = jnp.tril(jnp.ones((tc, tc), F32)).astype(BF16)
    out = jax.ShapeDtypeStruct((b, n_heads, l, w), BF16)
    ospec = pl.BlockSpec((1, n_heads, tc, w), lambda i, j: (i, 0, j, 0))
    return pl.pallas_call(
        functools.partial(_cumsum_kernel, n_heads=n_heads),
        grid=(b, l // tc),
        in_specs=[pl.BlockSpec((1, tc, w), lambda i, j: (i, j, 0)),
                  pl.BlockSpec((tc, tc), lambda i, j: (0, 0))],
        out_specs=[ospec, ospec],
        out_shape=[out, out],
        scratch_shapes=[pltpu.VMEM((1, w), F32)],
        compiler_params=_params("parallel", "arbitrary"), name="forget_cumsum",
    )(lf, tri)


def _fox_kernel(q_ref, qx_ref, k_ref, kx_ref, v_ref, g_ref, o_ref, s_sc, m_sc, acc_sc, *, tq, nq):
    acc_sc[...] = jnp.zeros_like(acc_sc)
    nrep = tq // HEAD_DIM
    ones = jnp.ones((tq, HEAD_DIM), BF16)
    contract_last = (((1,), (1,)), ((), ()))
    ahead = lax.broadcasted_iota(jnp.int32, (tq, tq), 1) - lax.broadcasted_iota(jnp.int32, (tq, tq), 0)

    def rows_of(blk):
        return pl.ds(pl.multiple_of(blk * tq, tq), tq)

    def scores(qi, kb, slot):
        q = jnp.concatenate([q_ref[0, rows_of(qi), :], qx_ref[0, 0, rows_of(qi), :]], axis=1)
        k = jnp.concatenate([k_ref[0, rows_of(kb), :], kx_ref[0, 0, rows_of(kb), :]], axis=1)
        s_sc[slot] = lax.dot_general(q, k, contract_last, preferred_element_type=F32)

    def accumulate(qi, kb, slot):
        v = jnp.concatenate([v_ref[0, rows_of(kb), :], ones], axis=1)
        s = jnp.where(ahead <= (qi - kb) * tq, s_sc[slot], NEG_BIG)
        m_prev = jnp.where(kb == 0, NEG_BIG, m_sc[...])
        m_new = jnp.maximum(m_prev, jnp.max(s, axis=1, keepdims=True))
        alpha = jnp.exp2(m_prev - m_new)
        p = jnp.exp2(s - jnp.tile(m_new, (1, nrep)))
        acc_sc[rows_of(qi), :] = (jnp.tile(alpha, (1, 2)) * acc_sc[rows_of(qi), :]
                                  + jnp.dot(p.astype(BF16), v, preferred_element_type=F32))
        m_sc[...] = m_new

    def step(tile, slot, prefetch):
        qi, kb = tile
        last = kb == qi
        nxt = (jnp.where(last, qi + 1, qi), jnp.where(last, 0, kb + 1))
        if prefetch:
            scores(jnp.minimum(nxt[0], nq - 1), nxt[1], 1 - slot)
        accumulate(qi, kb, slot)
        return nxt

    n_tiles = nq * (nq + 1) // 2
    unroll = 4
    scores(0, 0, 0)

    def body(i, tile):
        for j in range(unroll):
            tile = step(tile, j % 2, True)
        return tile

    tile = lax.fori_loop(0, n_tiles // unroll, body, (jnp.int32(0), jnp.int32(0)))
    left = n_tiles % unroll
    for j in range(left):
        tile = step(tile, j % 2, j + 1 < left)

    def finish(blk, carry):
        acc = acc_sc[rows_of(blk), :]
        gate = g_ref[0, rows_of(blk), :].astype(F32)
        o_ref[0, rows_of(blk), :] = (acc[:, :HEAD_DIM] / acc[:, HEAD_DIM:] * _silu(gate)).astype(o_ref.dtype)
        return carry

    lax.fori_loop(0, nq, finish, 0)


def _fox_attention(p3, qx, kx, n_heads, tq):
    b, l, _ = p3.shape
    nq = l // tq
    kern = functools.partial(_fox_kernel, tq=tq, nq=nq)
    col = lambda blk0: pl.BlockSpec((1, l, HEAD_DIM), lambda bi, h: (bi, 0, blk0 + h))
    per_head = pl.BlockSpec((1, 1, l, HEAD_DIM), lambda bi, h: (bi, h, 0, 0))
    return pl.pallas_call(
        kern,
        grid=(b, n_heads),
        in_specs=[col(0), per_head, col(n_heads), per_head, col(2 * n_heads), col(3 * n_heads)],
        out_specs=col(0),
        out_shape=jax.ShapeDtypeStruct((b, l, n_heads * HEAD_DIM), BF16),
        scratch_shapes=[pltpu.VMEM((2, tq, tq), F32), pltpu.VMEM((tq, HEAD_DIM), F32),
                        pltpu.VMEM((l, 2 * HEAD_DIM), F32)],
        compiler_params=_params("parallel", "arbitrary"), name="fox_attn",
    )(p3, qx, p3, kx, p3, p3)


def _sb_kernel(q_ref, k_ref, v_ref, g_ref, tri_ref, o_ref, cost_sc, acc_sc, *, tq, tk):
    qi = pl.program_id(2)
    ng = tq // tk
    cost_sc[...] = jnp.zeros_like(cost_sc)
    acc_sc[...] = jnp.zeros_like(acc_sc)

    def tile(r, kb, diagonal):
        rows = pl.ds(r * tk, tk)
        start = pl.multiple_of(kb * tk, tk)
        k = k_ref[0, pl.ds(start, tk), :]
        v = v_ref[0, pl.ds(start, tk), :]
        z = lax.dot_general(q_ref[0, rows, :], k, (((1,), (1,)), ((), ())), preferred_element_type=F32)
        softplus = jnp.maximum(z, 0.0) + jnp.log(1.0 + jnp.exp(-jnp.abs(z)))
        log_beta = z - softplus
        if diagonal:
            mask = (lax.broadcasted_iota(jnp.int32, (tk, tk), 1)
                    < lax.broadcasted_iota(jnp.int32, (tk, tk), 0))
            softplus = jnp.where(mask, softplus, 0.0)
        within = jnp.dot(softplus.astype(BF16), tri_ref[...], preferred_element_type=F32)
        w = jnp.exp(log_beta - within - jnp.tile(cost_sc[rows, :], (1, tk // HEAD_DIM)))
        if diagonal:
            w = jnp.where(mask, w, 0.0)
        acc_sc[rows, :] += jnp.dot(w.astype(BF16), v, preferred_element_type=F32)
        cost_sc[rows, :] += jnp.sum(softplus, axis=1, keepdims=True)

    for r in range(ng):
        tile(r, qi * ng + r, True)

    def live(state):
        return state[1] < SB_EXP_UNDERFLOW

    def body(state):
        for r in range(ng):
            tile(r, qi * ng + r - state[0], False)
        return state[0] + 1, jnp.min(cost_sc[...])

    state = lax.while_loop(lambda s: jnp.logical_and(s[0] <= qi * ng, live(s)), body,
                           (1, jnp.min(cost_sc[...])))

    def tail(state):
        for r in range(1, ng):
            kb = qi * ng + r - state[0]

            @pl.when(kb >= 0)
            def _():
                tile(r, kb, False)
        return state[0] + 1, jnp.min(cost_sc[...])

    lax.while_loop(lambda s: jnp.logical_and(s[0] <= qi * ng + ng - 1, live(s)), tail, state)
    gate = g_ref[0].astype(F32)
    o_ref[0] = (acc_sc[...] * _silu(gate)).astype(o_ref.dtype)


def _sb_attention(p3, col0, n_heads, tq, tk):
    b, l, _ = p3.shape
    nq = l // tq
    tri = jnp.tril(jnp.ones((tk, tk), F32), k=-1).astype(BF16)
    kern = functools.partial(_sb_kernel, tq=tq, tk=tk)
    return pl.pallas_call(
        kern,
        grid=(b, n_heads, nq),
        in_specs=[pl.BlockSpec((1, tq, HEAD_DIM), lambda bi, h, qi: (bi, qi, col0 + h)),
                  pl.BlockSpec((1, l, HEAD_DIM), lambda bi, h, qi: (bi, 0, col0 + n_heads + h)),
                  pl.BlockSpec((1, l, HEAD_DIM), lambda bi, h, qi: (bi, 0, col0 + 2 * n_heads + h)),
                  pl.BlockSpec((1, tq, HEAD_DIM), lambda bi, h, qi: (bi, qi, col0 + 3 * n_heads + h)),
                  pl.BlockSpec((tk, tk), lambda bi, h, qi: (0, 0))],
        out_specs=pl.BlockSpec((1, tq, HEAD_DIM), lambda bi, h, qi: (bi, qi, h)),
        out_shape=jax.ShapeDtypeStruct((b, l, n_heads * HEAD_DIM), BF16),
        scratch_shapes=[pltpu.VMEM((tq, HEAD_DIM), F32)] * 2,
        compiler_params=_params("parallel", "parallel", "arbitrary"), name="sb_attn",
    )(p3, p3, p3, p3, tri)


def _ssm_weights(a_re, a_im, log_dt, b_re, b_im, c_re, c_im):
    g, p = a_re.shape
    hg = b_re.shape[-1]
    c = SSM_CHUNK
    dt = jnp.exp(log_dt.astype(F32))[:, None]
    a_re = a_re.astype(F32)
    a_im = a_im.astype(F32)
    n = jnp.arange(c + 1, dtype=F32)[:, None, None]
    mag = jnp.exp(a_re * dt * n)
    ang = a_im * dt * n
    pw_re, pw_im = mag * jnp.cos(ang), mag * jnp.sin(ang)
    num_re, num_im = pw_re[1] - 1.0, pw_im[1]
    den = a_re * a_re + a_im * a_im
    f_re = (num_re * a_re + num_im * a_im) / den
    f_im = (num_im * a_re - num_re * a_im) / den
    bb_re = f_re[..., None] * b_re - f_im[..., None] * b_im
    bb_im = f_re[..., None] * b_im + f_im[..., None] * b_re
    cc_re, cc_im = c_re.astype(F32), c_im.astype(F32)
    hi = lax.Precision.HIGHEST
    cl_re = cc_re[None] * pw_re[:, :, None, :] - cc_im[None] * pw_im[:, :, None, :]
    cl_im = cc_re[None] * pw_im[:, :, None, :] + cc_im[None] * pw_re[:, :, None, :]
    kk = jnp.einsum('ngop,gpi->ngoi', jnp.concatenate([cl_re[:c], cl_im[:c]], axis=3),
                    jnp.concatenate([bb_re, -bb_im], axis=1), precision=hi)
    ug = SSM_UNIT // hg
    nu = g // ug

    def spread(x, inner):
        rows, w = x.shape
        y = jnp.dot(x.astype(BF16), jnp.tile(jnp.eye(w, dtype=BF16), (1, ug)), preferred_element_type=F32)
        row_group = (lax.broadcasted_iota(jnp.int32, y.shape, 0) // inner) % ug
        col_group = lax.broadcasted_iota(jnp.int32, y.shape, 1) // w
        return jnp.where(row_group == col_group, y, 0.0).astype(BF16)

    bd = spread(kk.transpose(0, 1, 3, 2).reshape(c * g * hg, hg), hg).reshape(c, nu, SSM_UNIT, SSM_UNIT)
    zero = jnp.zeros_like(bd[0])
    m = jnp.concatenate([jnp.concatenate([bd[t - s] if t >= s else zero for t in range(c)], axis=2)
                         for s in range(c)], axis=1)
    n_rev = jnp.arange(c - 1, -1, -1, dtype=F32)[:, None, None]
    pr = jnp.exp(a_re * dt * n_rev) * jnp.cos(a_im * dt * n_rev)
    pi_ = jnp.exp(a_re * dt * n_rev) * jnp.sin(a_im * dt * n_rev)
    be_re = pr[..., None] * bb_re[None] - pi_[..., None] * bb_im[None]
    be_im = pr[..., None] * bb_im[None] + pi_[..., None] * bb_re[None]

    def bend(x):
        x = x.reshape(c, nu, ug, p, hg).transpose(1, 0, 2, 4, 3)
        return spread(x.reshape(nu * c * SSM_UNIT, p), hg).reshape(nu, c * SSM_UNIT, ug * p)

    def cout(x):
        x = x.reshape(c, nu, ug, hg, p).transpose(0, 1, 2, 4, 3)
        x = spread(x.reshape(c * nu * ug * p, hg), p).reshape(c, nu, ug * p, SSM_UNIT)
        return jnp.concatenate([x[t] for t in range(c)], axis=2)

    lam_re = pw_re[c].reshape(nu, 1, ug * p)
    lam_im = pw_im[c].reshape(nu, 1, ug * p)
    return m, bend(be_re), bend(be_im), cout(cl_re[1:]), cout(-cl_im[1:]), lam_re, lam_im


def _ssm_kernel(x_ref, m_ref, bre_ref, bim_ref, cre_ref, cim_ref, lre_ref, lim_ref, y_ref,
                xf_sc, zre_sc, zim_sc, yf_sc, *, n_chunk):
    c = SSM_CHUNK
    xf_sc[...] = x_ref[0].astype(F32)
    u = jnp.concatenate([xf_sc[pl.ds(s, n_chunk, stride=c), :] for s in range(c)], axis=1).astype(BF16)
    zre_sc[...] = jnp.dot(u, bre_ref[0], preferred_element_type=F32)
    zim_sc[...] = jnp.dot(u, bim_ref[0], preferred_element_type=F32)
    lre = lre_ref[0]
    lim = lim_ref[0]

    def body(kc, carry):
        xr, xi = carry
        zr = zre_sc[pl.ds(kc, 1), :]
        zi = zim_sc[pl.ds(kc, 1), :]
        zre_sc[pl.ds(kc, 1), :] = xr
        zim_sc[pl.ds(kc, 1), :] = xi
        return lre * xr - lim * xi + zr, lre * xi + lim * xr + zi

    zero = jnp.zeros((1, zre_sc.shape[1]), F32)
    lax.fori_loop(0, n_chunk, body, (zero, zero))
    y = (jnp.dot(u, m_ref[0], preferred_element_type=F32)
         + jnp.dot(zre_sc[...].astype(BF16), cre_ref[0], preferred_element_type=F32)
         + jnp.dot(zim_sc[...].astype(BF16), cim_ref[0], preferred_element_type=F32))
    for t in range(c):
        yf_sc[pl.ds(t, n_chunk, stride=c), :] = y[:, t * SSM_UNIT:(t + 1) * SSM_UNIT]
    y_ref[0] = yf_sc[...].astype(y_ref.dtype)


def _ssm_scan(p3, x_blk0, weights, layer):
    b, l, _ = p3.shape
    m, bre, bim, cre, cim, lre, lim = weights
    nu = m.shape[1]
    lanes = m.shape[2]
    sp = bre.shape[3]
    nck = l // SSM_CHUNK
    kern = functools.partial(_ssm_kernel, n_chunk=nck)
    per_unit = lambda r, cc: pl.BlockSpec((None, 1, r, cc), lambda u, bi: (layer, u, 0, 0))
    return pl.pallas_call(
        kern,
        grid=(nu, b),
        in_specs=[pl.BlockSpec((1, l, SSM_UNIT), lambda u, bi: (bi, 0, x_blk0 + u)),
                  per_unit(lanes, lanes), per_unit(lanes, sp), per_unit(lanes, sp),
                  per_unit(sp, lanes), per_unit(sp, lanes), per_unit(1, sp), per_unit(1, sp)],
        out_specs=pl.BlockSpec((1, l, SSM_UNIT), lambda u, bi: (bi, 0, u)),
        out_shape=jax.ShapeDtypeStruct((b, l, nu * SSM_UNIT), BF16),
        scratch_shapes=[pltpu.VMEM((l, SSM_UNIT), F32), pltpu.VMEM((nck, sp), F32),
                        pltpu.VMEM((nck, sp), F32), pltpu.VMEM((l, SSM_UNIT), F32)],
        compiler_params=_params("parallel", "arbitrary"), name="ssm_scan",
    )(p3, m, bre, bim, cre, cim, lre, lim)


def _glu_kernel(y_ref, u_ref, g_ref, d_ref, w_ref, bg_ref, o_ref):
    y = y_ref[...].astype(F32) + d_ref[...] * u_ref[...].astype(F32)
    y = jax.nn.gelu(y)
    z = jnp.dot(y.astype(BF16), w_ref[...], preferred_element_type=F32) + bg_ref[...]
    o_ref[...] = (y * jax.nn.sigmoid(z) * _silu(g_ref[...].astype(F32))).astype(o_ref.dtype)


def _ssm_glu(y, p2, u_blk, gate_blk, d_skip, w_glu, b_glu, layer, tm):
    t, bw = y.shape
    return pl.pallas_call(
        _glu_kernel,
        grid=(t // tm,),
        in_specs=[pl.BlockSpec((tm, bw), lambda i: (i, 0)),
                  pl.BlockSpec((tm, bw), lambda i: (i, u_blk)),
                  pl.BlockSpec((tm, bw), lambda i: (i, gate_blk)),
                  pl.BlockSpec((1, bw), lambda i: (0, 0)),
                  pl.BlockSpec((None, bw, bw), lambda i: (layer, 0, 0)),
                  pl.BlockSpec((1, bw), lambda i: (0, 0))],
        out_specs=pl.BlockSpec((tm, bw), lambda i: (i, 0)),
        out_shape=jax.ShapeDtypeStruct((t, bw), BF16),
        compiler_params=_params("parallel"), name="ssm_glu",
    )(y, p2, p2, d_skip.reshape(1, bw).astype(F32), w_glu, b_glu.reshape(1, bw).astype(F32))


def _lru_kernel(x_ref, g_ref, cw_ref, cb_ref, wg_ref, br_ref, bi_ref, lam_ref, o_ref,
                tail_sc, h_sc, a_sc, b_sc, *, tl):
    @pl.when(pl.program_id(1) == 0)
    def _():
        tail_sc[...] = jnp.zeros_like(tail_sc)
        h_sc[...] = jnp.zeros_like(h_sc)

    x = x_ref[0].astype(F32)
    w = x.shape[1]
    xcat = jnp.concatenate([tail_sc[...], x], axis=0)
    xc = cb_ref[...] + cw_ref[CONV_WIDTH - 1:CONV_WIDTH, :] * x
    for d in range(1, CONV_WIDTH):
        xd = pltpu.roll(xcat, d, 0)[SUBLANES:, :]
        xc = xc + cw_ref[CONV_WIDTH - 1 - d:CONV_WIDTH - d, :] * xd
    tail_sc[...] = x[tl - SUBLANES:, :]

    xb = xc.astype(BF16)
    nblk = w // MXU_DIM
    gates = [jnp.dot(xb[:, kb * MXU_DIM:(kb + 1) * MXU_DIM], wg_ref[kb], preferred_element_type=F32)
             for kb in range(nblk)]
    gr = jnp.concatenate([gt[:, :MXU_DIM] for gt in gates], axis=1)
    gi = jnp.concatenate([gt[:, MXU_DIM:] for gt in gates], axis=1)
    r = jax.nn.sigmoid(gr + br_ref[...])
    i = jax.nn.sigmoid(gi + bi_ref[...])
    log_a = r * (LRU_C * _log_sigmoid(lam_ref[...]))
    a = jnp.exp(log_a)
    gap = 1.0 - a * a
    bx = gap * lax.rsqrt(jnp.maximum(gap, TINY)) * (i * xc)

    nt = tl // SUBLANES
    a = a.reshape(nt, SUBLANES, w)
    bx = bx.reshape(nt, SUBLANES, w)
    sub = lax.broadcasted_iota(jnp.int32, (nt, SUBLANES, w), 1)
    d = 1
    while d < SUBLANES:
        a_prev = pltpu.roll(a, d, 1)
        b_prev = pltpu.roll(bx, d, 1)
        use = sub >= d
        bx = jnp.where(use, a * b_prev + bx, bx)
        a = jnp.where(use, a * a_prev, a)
        d *= 2
    a_sc[...] = a.reshape(tl, w)
    b_sc[...] = bx.reshape(tl, w)

    def body(kt, h_last):
        s = pl.multiple_of(kt * SUBLANES, SUBLANES)
        ht = a_sc[pl.ds(s, SUBLANES), :] * h_last + b_sc[pl.ds(s, SUBLANES), :]
        b_sc[pl.ds(s, SUBLANES), :] = ht
        return ht[SUBLANES - 1:, :]

    h_sc[...] = lax.fori_loop(0, tl // SUBLANES, body, h_sc[...])
    o_ref[0] = (b_sc[...] * _silu(g_ref[0].astype(F32))).astype(o_ref.dtype)


def _lru_gate_tiles(w_r, w_i):
    depth, nb = w_r.shape[:2]
    per = MXU_DIM // LRU_BLOCK
    nblk = nb // per

    def tile_diag(wg):
        wg = wg.astype(F32).reshape(depth, nblk, per, LRU_BLOCK, LRU_BLOCK)
        eye = jnp.eye(per, dtype=F32)
        return jnp.einsum('dkaio,ab->dkaibo', wg, eye).reshape(depth, nblk, MXU_DIM, MXU_DIM)

    return jnp.concatenate([tile_diag(w_r), tile_diag(w_i)], axis=3).astype(BF16)


def _lru(p3, x_blk, gate_blk, conv_w, conv_b, wg, b_r, b_i, lam, layer, tl):
    b, l, _ = p3.shape
    bw = conv_w.shape[1]
    nblk = bw // MXU_DIM
    vec = lambda a: a.reshape(1, bw).astype(F32)
    kern = functools.partial(_lru_kernel, tl=tl)
    full = lambda shape: pl.BlockSpec(shape, lambda bi, ti: (0,) * len(shape))
    return pl.pallas_call(
        kern,
        grid=(b, l // tl),
        in_specs=[pl.BlockSpec((1, tl, bw), lambda bi, ti: (bi, ti, x_blk)),
                  pl.BlockSpec((1, tl, bw), lambda bi, ti: (bi, ti, gate_blk)),
                  full((CONV_WIDTH, bw)), full((1, bw)),
                  pl.BlockSpec((None, nblk, MXU_DIM, 2 * MXU_DIM), lambda bi, ti: (layer, 0, 0, 0)),
                  full((1, bw)), full((1, bw)), full((1, bw))],
        out_specs=pl.BlockSpec((1, tl, bw), lambda bi, ti: (bi, ti, 0)),
        out_shape=jax.ShapeDtypeStruct((b, l, bw), BF16),
        scratch_shapes=[pltpu.VMEM((SUBLANES, bw), F32), pltpu.VMEM((1, bw), F32),
                        pltpu.VMEM((tl, bw), F32), pltpu.VMEM((tl, bw), F32)],
        compiler_params=_params("parallel", "arbitrary"), name="rg_lru",
    )(p3, p3, conv_w.astype(F32), vec(conv_b), wg, vec(b_r), vec(b_i), vec(lam))


def _tiles(b, l):
    t = b * l
    return dict(
        norm=min(256, t),
        mm_m=min(1024, t),
        mm_n=1024,
        cum=min(512, l),
        fox=min(512, l),
        sb_q=min(1024, l),
        sb_k=min(MXU_DIM, l),
        glu=min(512, t),
        lru=min(512, l),
    )


def kernel(x, g_pre, w_in, b_forget, ssm_a_re, ssm_a_im, ssm_log_dt, ssm_b_re, ssm_b_im, ssm_c_re, ssm_c_im, ssm_d, w_glu, b_glu, conv_w, conv_b, w_rgate, b_rgate, w_igate, b_igate, lru_lambda, w_out, g_post):
    b, l, d = x.shape
    depth = w_in.shape[0]
    bw = d // 4
    nh = bw // HEAD_DIM
    t = b * l
    ts = _tiles(b, l)
    assert w_in.shape[2] == 12 * bw + nh and l % SSM_CHUNK == 0

    def scaled_q(w, nblk, scale):
        sc = jnp.ones((nblk,), F32).at[0].set(scale)
        return (w.reshape(depth, d, nblk, bw) * sc[None, None, :, None]).reshape(depth, d, nblk * bw).astype(BF16)

    w_fox = scaled_q(w_in[:, :, :4 * bw], 4, HEAD_DIM ** -0.5 * LOG2E)
    w_rest = scaled_q(w_in[:, :, 4 * bw + nh:], 8, HEAD_DIM ** -0.5)
    w_f = jnp.pad(w_in[:, :, 4 * bw:4 * bw + nh], ((0, 0), (0, 0), (0, HEAD_DIM - nh))).astype(BF16)
    b_f = jnp.pad(b_forget.astype(F32), ((0, 0), (0, HEAD_DIM - nh))).reshape(depth, 1, HEAD_DIM)
    ssm_w = jax.vmap(_ssm_weights)(ssm_a_re, ssm_a_im, ssm_log_dt, ssm_b_re, ssm_b_im, ssm_c_re, ssm_c_im)
    lru_wg = _lru_gate_tiles(w_rgate, w_igate)
    w_glu_b = w_glu.astype(BF16)
    w_out_b = w_out.astype(BF16)

    h = x.reshape(t, d)
    u = _rmsnorm(h, g_pre[0], ts["norm"])
    for layer in range(depth):
        pf, log_f = _inproj(u, w_fox, 4 * bw, w_f, b_f, layer, ts["mm_m"], ts["mm_n"])
        pr = _inproj_plain(u, w_rest, layer, ts["mm_m"], ts["mm_n"])
        pf3 = pf.reshape(b, l, 4 * bw)
        pr3 = pr.reshape(b, l, 8 * bw)

        qx, kx = _forget_columns(log_f.reshape(b, l, HEAD_DIM), nh, ts["cum"])
        o_fox = _fox_attention(pf3, qx, kx, nh, ts["fox"])
        o_sb = _sb_attention(pr3, 0, nh, ts["sb_q"], ts["sb_k"])

        y_ssm = _ssm_scan(pr3, 4 * bw // SSM_UNIT, ssm_w, layer)
        o_ssm = _ssm_glu(y_ssm.reshape(t, bw), pr, 4, 5, ssm_d[layer], w_glu_b, b_glu[layer], layer, ts["glu"])

        o_lru = _lru(pr3, 6, 7, conv_w[layer], conv_b[layer], lru_wg, b_rgate[layer], b_igate[layer],
                     lru_lambda[layer], layer, ts["lru"])

        y = _outproj([o_fox.reshape(t, bw), o_sb.reshape(t, bw), o_ssm, o_lru.reshape(t, bw)],
                     w_out_b, layer, ts["mm_m"], ts["mm_n"])
        g_next = g_pre[layer + 1] if layer + 1 < depth else None
        h, u = _post(y, h, g_post[layer], g_next, ts["norm"])
    return h.reshape(b, l, d)
```

```python
import functools
import math

import jax
import jax.numpy as jnp
from jax import lax
from jax.experimental import pallas as pl
from jax.experimental.pallas import tpu as pltpu

F32 = jnp.float32
BF16 = jnp.bfloat16

HEAD_DIM = 128
SSM_GROUP = 16
SSM_STATE = 64
SSM_UNIT = 128
SSM_CHUNK = 8
LRU_BLOCK = 64
LRU_C = 8.0
CONV_WIDTH = 4
RMS_EPS = 1e-6
NEG_BIG = -1e30
LOG2E = math.log2(math.e)
TINY = 1e-37
SB_EXP_UNDERFLOW = 110.0
SUBLANES = 8
MXU_DIM = 256
VMEM_LIMIT = 56 * 1024 * 1024


def _params(*sem):
    return pltpu.CompilerParams(dimension_semantics=sem, vmem_limit_bytes=VMEM_LIMIT)


def _log_sigmoid(x):
    return jnp.minimum(x, 0.0) - jnp.log1p(jnp.exp(-jnp.abs(x)))


def _silu(x):
    return x * jax.nn.sigmoid(x)


def _rmsnorm_kernel(x_ref, g_ref, u_ref):
    x = x_ref[...]
    ms = jnp.mean(x * x, axis=-1, keepdims=True)
    u_ref[...] = (x * lax.rsqrt(ms + RMS_EPS) * g_ref[...]).astype(u_ref.dtype)


def _rmsnorm(x, g, tm):
    t, d = x.shape
    return pl.pallas_call(
        _rmsnorm_kernel,
        grid=(t // tm,),
        in_specs=[pl.BlockSpec((tm, d), lambda i: (i, 0)),
                  pl.BlockSpec((1, d), lambda i: (0, 0))],
        out_specs=pl.BlockSpec((tm, d), lambda i: (i, 0)),
        out_shape=jax.ShapeDtypeStruct((t, d), BF16),
        compiler_params=_params("parallel"), name="rmsnorm_first",
    )(x, g.reshape(1, d))


def _post_kernel(y_ref, h_ref, gpost_ref, gnext_ref, hn_ref, u_ref):
    y = y_ref[...].astype(F32)
    ms = jnp.mean(y * y, axis=-1, keepdims=True)
    hn = h_ref[...] + y * lax.rsqrt(ms + RMS_EPS) * gpost_ref[...]
    hn_ref[...] = hn
    ms2 = jnp.mean(hn * hn, axis=-1, keepdims=True)
    u_ref[...] = (hn * lax.rsqrt(ms2 + RMS_EPS) * gnext_ref[...]).astype(u_ref.dtype)


def _post_last_kernel(y_ref, h_ref, gpost_ref, hn_ref):
    y = y_ref[...].astype(F32)
    ms = jnp.mean(y * y, axis=-1, keepdims=True)
    hn_ref[...] = h_ref[...] + y * lax.rsqrt(ms + RMS_EPS) * gpost_ref[...]


def _post(y, h, g_post, g_next, tm):
    t, d = y.shape
    row = pl.BlockSpec((tm, d), lambda i: (i, 0))
    vec = pl.BlockSpec((1, d), lambda i: (0, 0))
    if g_next is None:
        return pl.pallas_call(
            _post_last_kernel, grid=(t // tm,),
            in_specs=[row, row, vec], out_specs=row,
            out_shape=jax.ShapeDtypeStruct((t, d), F32),
            compiler_params=_params("parallel"), name="post_norm_last",
        )(y, h, g_post.reshape(1, d)), None
    return pl.pallas_call(
        _post_kernel, grid=(t // tm,),
        in_specs=[row, row, vec, vec], out_specs=[row, row],
        out_shape=[jax.ShapeDtypeStruct((t, d), F32), jax.ShapeDtypeStruct((t, d), BF16)],
        compiler_params=_params("parallel"), name="post_norm",
    )(y, h, g_post.reshape(1, d), g_next.reshape(1, d))


def _inproj_kernel(u_ref, w_ref, wf_ref, bf_ref, p_ref, lf_ref):
    u = u_ref[...]
    p_ref[...] = jnp.dot(u, w_ref[...], preferred_element_type=F32).astype(p_ref.dtype)

    @pl.when(pl.program_id(1) == 0)
    def _():
        logit = jnp.dot(u, wf_ref[...], preferred_element_type=F32) + bf_ref[...]
        lf_ref[...] = _log_sigmoid(logit)


def _matmul_kernel(u_ref, w_ref, p_ref):
    p_ref[...] = jnp.dot(u_ref[...], w_ref[...], preferred_element_type=F32).astype(p_ref.dtype)


def _inproj_plain(u, w, layer, tm, tn):
    t, d = u.shape
    n = w.shape[2]
    return pl.pallas_call(
        _matmul_kernel,
        grid=(t // tm, n // tn),
        in_specs=[pl.BlockSpec((tm, d), lambda i, j: (i, 0)),
                  pl.BlockSpec((None, d, tn), lambda i, j: (layer, 0, j))],
        out_specs=pl.BlockSpec((tm, tn), lambda i, j: (i, j)),
        out_shape=jax.ShapeDtypeStruct((t, n), BF16),
        compiler_params=_params("parallel", "arbitrary"), name="in_proj_rest",
    )(u, w)


def _inproj(u, w, n, wf, bf, layer, tm, tn):
    t, d = u.shape
    return pl.pallas_call(
        _inproj_kernel,
        grid=(t // tm, n // tn),
        in_specs=[pl.BlockSpec((tm, d), lambda i, j: (i, 0)),
                  pl.BlockSpec((None, d, tn), lambda i, j: (layer, 0, j)),
                  pl.BlockSpec((None, d, HEAD_DIM), lambda i, j: (layer, 0, 0)),
                  pl.BlockSpec((None, 1, HEAD_DIM), lambda i, j: (layer, 0, 0))],
        out_specs=[pl.BlockSpec((tm, tn), lambda i, j: (i, j)),
                   pl.BlockSpec((tm, HEAD_DIM), lambda i, j: (i, 0))],
        out_shape=[jax.ShapeDtypeStruct((t, n), BF16),
                   jax.ShapeDtypeStruct((t, HEAD_DIM), F32)],
        compiler_params=_params("parallel", "arbitrary"), name="in_proj",
    )(u, w, wf, bf)


def _outproj_kernel(a_ref, b_ref, c_ref, d_ref, w_ref, y_ref):
    x = jnp.concatenate([a_ref[...], b_ref[...], c_ref[...], d_ref[...]], axis=1)
    y_ref[...] = jnp.dot(x, w_ref[...], preferred_element_type=F32).astype(y_ref.dtype)


def _outproj(branches, w, layer, tm, tn):
    t, bw = branches[0].shape
    _, k, n = w.shape
    bspec = pl.BlockSpec((tm, bw), lambda i, j: (i, 0))
    return pl.pallas_call(
        _outproj_kernel,
        grid=(t // tm, n // tn),
        in_specs=[bspec, bspec, bspec, bspec, pl.BlockSpec((None, k, tn), lambda i, j: (layer, 0, j))],
        out_specs=pl.BlockSpec((tm, tn), lambda i, j: (i, j)),
        out_shape=jax.ShapeDtypeStruct((t, n), BF16),
        compiler_params=_params("parallel", "arbitrary"), name="out_proj",
    )(*branches, w)


def _split3(x):
    hi = x.astype(BF16).astype(F32)
    r1 = x - hi
    mid = r1.astype(BF16).astype(F32)
    lo = (r1 - mid).astype(BF16).astype(F32)
    return hi, mid, lo


def _cumsum_kernel(lf_ref, tri_ref, qx_ref, kx_ref, carry_ref, *, n_heads):
    @pl.when(pl.program_id(1) == 0)
    def _():
        carry_ref[...] = jnp.zeros_like(carry_ref)

    hi, mid, lo = _split3(lf_ref[0])
    tri = tri_ref[...]
    c = (jnp.dot(tri, hi.astype(BF16), preferred_element_type=F32)
         + jnp.dot(tri, mid.astype(BF16), preferred_element_type=F32)
         + jnp.dot(tri, lo.astype(BF16), preferred_element_type=F32)) + carry_ref[...]
    carry_ref[...] = c[-1:, :]

    c2 = c * LOG2E
    lane = lax.broadcasted_iota(jnp.int32, c2.shape, 1)
    ones = jnp.where(lane < 3, 1.0, 0.0)
    for h in range(n_heads):
        f = jnp.sum(jnp.where(lane == h, c2, 0.0), axis=1, keepdims=True)
        hi, mid, lo = _split3(f)
        pieces = jnp.where(lane == 0, hi, jnp.where(lane == 1, mid, jnp.where(lane == 2, lo, 0.0)))
        qx_ref[0, h] = (pieces + pltpu.roll(ones, 3, 1)).astype(qx_ref.dtype)
        kx_ref[0, h] = (ones - pltpu.roll(pieces, 3, 1)).astype(kx_ref.dtype)


def _forget_columns(lf, n_heads, tc):
    b, l, w = lf.shape
    tri = jnp.tril(jnp.ones((tc, tc), F32)).astype(BF16)
    out = jax.ShapeDtypeStruct((b, n_heads, l, w), BF16)
    ospec = pl.BlockSpec((1, n_heads, tc, w), lambda i, j: (i, 0, j, 0))
    return pl.pallas_call(
        functools.partial(_cumsum_kernel, n_heads=n_heads),
        grid=(b, l // tc),
        in_specs=[pl.BlockSpec((1, tc, w), lambda i, j: (i, j, 0)),
                  pl.BlockSpec((tc, tc), lambda i, j: (0, 0))],
        out_specs=[ospec, ospec],
        out_shape=[out, out],
        scratch_shapes=[pltpu.VMEM((1, w), F32)],
        compiler_params=_params("parallel", "arbitrary"), name="forget_cumsum",
    )(lf, tri)


def _fox_kernel(q_ref, qx_ref, k_ref, kx_ref, v_ref, g_ref, o_ref, s_sc, m_sc, acc_sc, *, tq, nq):
    acc_sc[...] = jnp.zeros_like(acc_sc)
    nrep = tq // HEAD_DIM
    ones = jnp.ones((tq, HEAD_DIM), BF16)
    contract_last = (((1,), (1,)), ((), ()))
    ahead = lax.broadcasted_iota(jnp.int32, (tq, tq), 1) - lax.broadcasted_iota(jnp.int32, (tq, tq), 0)

    def rows_of(blk):
        return pl.ds(pl.multiple_of(blk * tq, tq), tq)

    def scores(qi, kb, slot):
        q = jnp.concatenate([q_ref[0, rows_of(qi), :], qx_ref[0, 0, rows_of(qi), :]], axis=1)
        k = jnp.concatenate([k_ref[0, rows_of(kb), :], kx_ref[0, 0, rows_of(kb), :]], axis=1)
        s_sc[slot] = lax.dot_general(q, k, contract_last, preferred_element_type=F32)

    def accumulate(qi, kb, slot):
        v = jnp.concatenate([v_ref[0, rows_of(kb), :], ones], axis=1)
        s = jnp.where(ahead <= (qi - kb) * tq, s_sc[slot], NEG_BIG)
        m_prev = jnp.where(kb == 0, NEG_BIG, m_sc[...])
        m_new = jnp.maximum(m_prev, jnp.max(s, axis=1, keepdims=True))
        alpha = jnp.exp2(m_prev - m_new)
        p = jnp.exp2(s - jnp.tile(m_new, (1, nrep)))
        acc_sc[rows_of(qi), :] = (jnp.tile(alpha, (1, 2)) * acc_sc[rows_of(qi), :]
                                  + jnp.dot(p.astype(BF16), v, preferred_element_type=F32))
        m_sc[...] = m_new

    def step(tile, slot, prefetch):
        qi, kb = tile
        last = kb == qi
        nxt = (jnp.where(last, qi + 1, qi), jnp.where(last, 0, kb + 1))
        if prefetch:
            scores(jnp.minimum(nxt[0], nq - 1), nxt[1], 1 - slot)
        accumulate(qi, kb, slot)
        return nxt

    n_tiles = nq * (nq + 1) // 2
    unroll = 4
    scores(0, 0, 0)

    def body(i, tile):
        for j in range(unroll):
            tile = step(tile, j % 2, True)
        return tile

    tile = lax.fori_loop(0, n_tiles // unroll, body, (jnp.int32(0), jnp.int32(0)))
    left = n_tiles % unroll
    for j in range(left):
        tile = step(tile, j % 2, j + 1 < left)

    def finish(blk, carry):
        acc = acc_sc[rows_of(blk), :]
        gate = g_ref[0, rows_of(blk), :].astype(F32)
        o_ref[0, rows_of(blk), :] = (acc[:, :HEAD_DIM] / acc[:, HEAD_DIM:] * _silu(gate)).astype(o_ref.dtype)
        return carry

    lax.fori_loop(0, nq, finish, 0)


def _fox_attention(p3, qx, kx, n_heads, tq):
    b, l, _ = p3.shape
    nq = l // tq
    kern = functools.partial(_fox_kernel, tq=tq, nq=nq)
    col = lambda blk0: pl.BlockSpec((1, l, HEAD_DIM), lambda bi, h: (bi, 0, blk0 + h))
    per_head = pl.BlockSpec((1, 1, l, HEAD_DIM), lambda bi, h: (bi, h, 0, 0))
    return pl.pallas_call(
        kern,
        grid=(b, n_heads),
        in_specs=[col(0), per_head, col(n_heads), per_head, col(2 * n_heads), col(3 * n_heads)],
        out_specs=col(0),
        out_shape=jax.ShapeDtypeStruct((b, l, n_heads * HEAD_DIM), BF16),
        scratch_shapes=[pltpu.VMEM((2, tq, tq), F32), pltpu.VMEM((tq, HEAD_DIM), F32),
                        pltpu.VMEM((l, 2 * HEAD_DIM), F32)],
        compiler_params=_params("parallel", "arbitrary"), name="fox_attn",
    )(p3, qx, p3, kx, p3, p3)


def _sb_kernel(q_ref, k_ref, v_ref, g_ref, tri_ref, o_ref, z_sc, cost_sc, acc_sc, *, tq, tk):
    qi = pl.program_id(2)
    ng = tq // tk
    cost_sc[...] = jnp.zeros_like(cost_sc)
    acc_sc[...] = jnp.zeros_like(acc_sc)
    groups = [pl.ds(r * tk, tk) for r in range(ng)]
    last_step = qi * ng + ng - 1

    def key_rows(r, n):
        kb = jnp.maximum(qi * ng + r - n, 0)
        return pl.ds(pl.multiple_of(kb * tk, tk), tk)

    def logits(n, slot):
        for r, rows in enumerate(groups):
            z_sc[slot, rows, :] = lax.dot_general(q_ref[0, rows, :], k_ref[0, key_rows(r, n), :],
                                                  (((1,), (1,)), ((), ())), preferred_element_type=F32)

    def process(n, slot, diagonal):
        for r, rows in enumerate(groups):
            z = z_sc[slot, rows, :]
            softplus = jnp.maximum(z, 0.0) + jnp.log(1.0 + jnp.exp(-jnp.abs(z)))
            log_beta = z - softplus
            if diagonal:
                mask = (lax.broadcasted_iota(jnp.int32, (tk, tk), 1)
                        < lax.broadcasted_iota(jnp.int32, (tk, tk), 0))
                softplus = jnp.where(mask, softplus, 0.0)
            within = jnp.dot(softplus.astype(BF16), tri_ref[...], preferred_element_type=F32)
            w = jnp.exp(log_beta - within - jnp.tile(cost_sc[rows, :], (1, tk // HEAD_DIM)))
            if diagonal:
                w = jnp.where(mask, w, 0.0)
            pv = jnp.dot(w.astype(BF16), v_ref[0, key_rows(r, n), :], preferred_element_type=F32)
            spent = jnp.sum(softplus, axis=1, keepdims=True)
            if not diagonal:
                has_keys = (qi * ng + r - n >= 0).astype(F32)
                pv = pv * has_keys
                spent = spent * has_keys
            acc_sc[rows, :] += pv
            cost_sc[rows, :] += spent

    logits(0, 0)
    logits(1, 1)
    process(0, 0, True)
    logits(2, 0)
    process(1, 1, False)

    def cond(state):
        return jnp.logical_and(state[0] <= last_step, state[1] < SB_EXP_UNDERFLOW)

    def body(state):
        n = state[0]
        logits(n + 1, 1)
        process(n, 0, False)
        logits(n + 2, 0)
        process(n + 1, 1, False)
        return n + 2, jnp.min(cost_sc[...])

    lax.while_loop(cond, body, (2, jnp.min(cost_sc[...])))
    gate = g_ref[0].astype(F32)
    o_ref[0] = (acc_sc[...] * _silu(gate)).astype(o_ref.dtype)


def _sb_attention(p3, col0, n_heads, tq, tk):
    b, l, _ = p3.shape
    nq = l // tq
    tri = jnp.tril(jnp.ones((tk, tk), F32), k=-1).astype(BF16)
    kern = functools.partial(_sb_kernel, tq=tq, tk=tk)
    return pl.pallas_call(
        kern,
        grid=(b, n_heads, nq),
        in_specs=[pl.BlockSpec((1, tq, HEAD_DIM), lambda bi, h, qi: (bi, qi, col0 + h)),
                  pl.BlockSpec((1, l, HEAD_DIM), lambda bi, h, qi: (bi, 0, col0 + n_heads + h)),
                  pl.BlockSpec((1, l, HEAD_DIM), lambda bi, h, qi: (bi, 0, col0 + 2 * n_heads + h)),
                  pl.BlockSpec((1, tq, HEAD_DIM), lambda bi, h, qi: (bi, qi, col0 + 3 * n_heads + h)),
                  pl.BlockSpec((tk, tk), lambda bi, h, qi: (0, 0))],
        out_specs=pl.BlockSpec((1, tq, HEAD_DIM), lambda bi, h, qi: (bi, qi, h)),
        out_shape=jax.ShapeDtypeStruct((b, l, n_heads * HEAD_DIM), BF16),
        scratch_shapes=[pltpu.VMEM((2, tq, tk), F32), pltpu.VMEM((tq, HEAD_DIM), F32),
                        pltpu.VMEM((tq, HEAD_DIM), F32)],
        compiler_params=_params("parallel", "parallel", "arbitrary"), name="sb_attn",
    )(p3, p3, p3, p3, tri)


def _ssm_weights(a_re, a_im, log_dt, b_re, b_im, c_re, c_im):
    g, p = a_re.shape
    hg = b_re.shape[-1]
    c = SSM_CHUNK
    dt = jnp.exp(log_dt.astype(F32))[:, None]
    a_re = a_re.astype(F32)
    a_im = a_im.astype(F32)
    n = jnp.arange(c + 1, dtype=F32)[:, None, None]
    mag = jnp.exp(a_re * dt * n)
    ang = a_im * dt * n
    pw_re, pw_im = mag * jnp.cos(ang), mag * jnp.sin(ang)
    num_re, num_im = pw_re[1] - 1.0, pw_im[1]
    den = a_re * a_re + a_im * a_im
    f_re = (num_re * a_re + num_im * a_im) / den
    f_im = (num_im * a_re - num_re * a_im) / den
    bb_re = f_re[..., None] * b_re - f_im[..., None] * b_im
    bb_im = f_re[..., None] * b_im + f_im[..., None] * b_re
    cc_re, cc_im = c_re.astype(F32), c_im.astype(F32)
    hi = lax.Precision.HIGHEST
    cl_re = cc_re[None] * pw_re[:, :, None, :] - cc_im[None] * pw_im[:, :, None, :]
    cl_im = cc_re[None] * pw_im[:, :, None, :] + cc_im[None] * pw_re[:, :, None, :]
    kk = jnp.einsum('ngop,gpi->ngoi', jnp.concatenate([cl_re[:c], cl_im[:c]], axis=3),
                    jnp.concatenate([bb_re, -bb_im], axis=1), precision=hi)
    ug = SSM_UNIT // hg
    nu = g // ug

    def spread(x, inner):
        rows, w = x.shape
        y = jnp.dot(x.astype(BF16), jnp.tile(jnp.eye(w, dtype=BF16), (1, ug)), preferred_element_type=F32)
        row_group = (lax.broadcasted_iota(jnp.int32, y.shape, 0) // inner) % ug
        col_group = lax.broadcasted_iota(jnp.int32, y.shape, 1) // w
        return jnp.where(row_group == col_group, y, 0.0).astype(BF16)

    bd = spread(kk.transpose(0, 1, 3, 2).reshape(c * g * hg, hg), hg).reshape(c, nu, SSM_UNIT, SSM_UNIT)
    zero = jnp.zeros_like(bd[0])
    m = jnp.concatenate([jnp.concatenate([bd[t - s] if t >= s else zero for t in range(c)], axis=2)
                         for s in range(c)], axis=1)
    n_rev = jnp.arange(c - 1, -1, -1, dtype=F32)[:, None, None]
    pr = jnp.exp(a_re * dt * n_rev) * jnp.cos(a_im * dt * n_rev)
    pi_ = jnp.exp(a_re * dt * n_rev) * jnp.sin(a_im * dt * n_rev)
    be_re = pr[..., None] * bb_re[None] - pi_[..., None] * bb_im[None]
    be_im = pr[..., None] * bb_im[None] + pi_[..., None] * bb_re[None]

    def bend(x):
        x = x.reshape(c, nu, ug, p, hg).transpose(1, 0, 2, 4, 3)
        return spread(x.reshape(nu * c * SSM_UNIT, p), hg).reshape(nu, c * SSM_UNIT, ug * p)

    def cout(x):
        x = x.reshape(c, nu, ug, hg, p).transpose(0, 1, 2, 4, 3)
        x = spread(x.reshape(c * nu * ug * p, hg), p).reshape(c, nu, ug * p, SSM_UNIT)
        return jnp.concatenate([x[t] for t in range(c)], axis=2)

    lam_re = pw_re[c].reshape(nu, 1, ug * p)
    lam_im = pw_im[c].reshape(nu, 1, ug * p)
    return m, bend(be_re), bend(be_im), cout(cl_re[1:]), cout(-cl_im[1:]), lam_re, lam_im


def _ssm_kernel(x_ref, m_ref, bre_ref, bim_ref, cre_ref, cim_ref, lre_ref, lim_ref, y_ref,
                xf_sc, zre_sc, zim_sc, yf_sc, *, n_chunk):
    c = SSM_CHUNK
    xf_sc[...] = x_ref[0].astype(F32)
    u = jnp.concatenate([xf_sc[pl.ds(s, n_chunk, stride=c), :] for s in range(c)], axis=1).astype(BF16)
    zre_sc[...] = jnp.dot(u, bre_ref[0], preferred_element_type=F32)
    zim_sc[...] = jnp.dot(u, bim_ref[0], preferred_element_type=F32)
    lre = lre_ref[0]
    lim = lim_ref[0]

    def body(kc, carry):
        xr, xi = carry
        zr = zre_sc[pl.ds(kc, 1), :]
        zi = zim_sc[pl.ds(kc, 1), :]
        zre_sc[pl.ds(kc, 1), :] = xr
        zim_sc[pl.ds(kc, 1), :] = xi
        return lre * xr - lim * xi + zr, lre * xi + lim * xr + zi

    zero = jnp.zeros((1, zre_sc.shape[1]), F32)
    lax.fori_loop(0, n_chunk, body, (zero, zero))
    y = (jnp.dot(u, m_ref[0], preferred_element_type=F32)
         + jnp.dot(zre_sc[...].astype(BF16), cre_ref[0], preferred_element_type=F32)
         + jnp.dot(zim_sc[...].astype(BF16), cim_ref[0], preferred_element_type=F32))
    for t in range(c):
        yf_sc[pl.ds(t, n_chunk, stride=c), :] = y[:, t * SSM_UNIT:(t + 1) * SSM_UNIT]
    y_ref[0] = yf_sc[...].astype(y_ref.dtype)


def _ssm_scan(p3, x_blk0, weights, layer):
    b, l, _ = p3.shape
    m, bre, bim, cre, cim, lre, lim = weights
    nu = m.shape[1]
    lanes = m.shape[2]
    sp = bre.shape[3]
    nck = l // SSM_CHUNK
    kern = functools.partial(_ssm_kernel, n_chunk=nck)
    per_unit = lambda r, cc: pl.BlockSpec((None, 1, r, cc), lambda u, bi: (layer, u, 0, 0))
    return pl.pallas_call(
        kern,
        grid=(nu, b),
        in_specs=[pl.BlockSpec((1, l, SSM_UNIT), lambda u, bi: (bi, 0, x_blk0 + u)),
                  per_unit(lanes, lanes), per_unit(lanes, sp), per_unit(lanes, sp),
                  per_unit(sp, lanes), per_unit(sp, lanes), per_unit(1, sp), per_unit(1, sp)],
        out_specs=pl.BlockSpec((1, l, SSM_UNIT), lambda u, bi: (bi, 0, u)),
        out_shape=jax.ShapeDtypeStruct((b, l, nu * SSM_UNIT), BF16),
        scratch_shapes=[pltpu.VMEM((l, SSM_UNIT), F32), pltpu.VMEM((nck, sp), F32),
                        pltpu.VMEM((nck, sp), F32), pltpu.VMEM((l, SSM_UNIT), F32)],
        compiler_params=_params("parallel", "arbitrary"), name="ssm_scan",
    )(p3, m, bre, bim, cre, cim, lre, lim)


def _glu_kernel(y_ref, u_ref, g_ref, d_ref, w_ref, bg_ref, o_ref):
    y = y_ref[...].astype(F32) + d_ref[...] * u_ref[...].astype(F32)
    y = jax.nn.gelu(y)
    z = jnp.dot(y.astype(BF16), w_ref[...], preferred_element_type=F32) + bg_ref[...]
    o_ref[...] = (y * jax.nn.sigmoid(z) * _silu(g_ref[...].astype(F32))).astype(o_ref.dtype)


def _ssm_glu(y, p2, u_blk, gate_blk, d_skip, w_glu, b_glu, layer, tm):
    t, bw = y.shape
    return pl.pallas_call(
        _glu_kernel,
        grid=(t // tm,),
        in_specs=[pl.BlockSpec((tm, bw), lambda i: (i, 0)),
                  pl.BlockSpec((tm, bw), lambda i: (i, u_blk)),
                  pl.BlockSpec((tm, bw), lambda i: (i, gate_blk)),
                  pl.BlockSpec((1, bw), lambda i: (0, 0)),
                  pl.BlockSpec((None, bw, bw), lambda i: (layer, 0, 0)),
                  pl.BlockSpec((1, bw), lambda i: (0, 0))],
        out_specs=pl.BlockSpec((tm, bw), lambda i: (i, 0)),
        out_shape=jax.ShapeDtypeStruct((t, bw), BF16),
        compiler_params=_params("parallel"), name="ssm_glu",
    )(y, p2, p2, d_skip.reshape(1, bw).astype(F32), w_glu, b_glu.reshape(1, bw).astype(F32))


def _lru_kernel(x_ref, g_ref, cw_ref, cb_ref, wg_ref, br_ref, bi_ref, lam_ref, o_ref,
                tail_sc, h_sc, a_sc, b_sc, *, tl):
    @pl.when(pl.program_id(1) == 0)
    def _():
        tail_sc[...] = jnp.zeros_like(tail_sc)
        h_sc[...] = jnp.zeros_like(h_sc)

    x = x_ref[0].astype(F32)
    w = x.shape[1]
    xcat = jnp.concatenate([tail_sc[...], x], axis=0)
    xc = cb_ref[...] + cw_ref[CONV_WIDTH - 1:CONV_WIDTH, :] * x
    for d in range(1, CONV_WIDTH):
        xd = pltpu.roll(xcat, d, 0)[SUBLANES:, :]
        xc = xc + cw_ref[CONV_WIDTH - 1 - d:CONV_WIDTH - d, :] * xd
    tail_sc[...] = x[tl - SUBLANES:, :]

    xb = xc.astype(BF16)
    nblk = w // MXU_DIM
    gates = [jnp.dot(xb[:, kb * MXU_DIM:(kb + 1) * MXU_DIM], wg_ref[kb], preferred_element_type=F32)
             for kb in range(nblk)]
    gr = jnp.concatenate([gt[:, :MXU_DIM] for gt in gates], axis=1)
    gi = jnp.concatenate([gt[:, MXU_DIM:] for gt in gates], axis=1)
    r = jax.nn.sigmoid(gr + br_ref[...])
    i = jax.nn.sigmoid(gi + bi_ref[...])
    log_a = r * (LRU_C * _log_sigmoid(lam_ref[...]))
    a = jnp.exp(log_a)
    gap = 1.0 - a * a
    bx = gap * lax.rsqrt(jnp.maximum(gap, TINY)) * (i * xc)

    nt = tl // SUBLANES
    a = a.reshape(nt, SUBLANES, w)
    bx = bx.reshape(nt, SUBLANES, w)
    sub = lax.broadcasted_iota(jnp.int32, (nt, SUBLANES, w), 1)
    d = 1
    while d < SUBLANES:
        a_prev = pltpu.roll(a, d, 1)
        b_prev = pltpu.roll(bx, d, 1)
        use = sub >= d
        bx = jnp.where(use, a * b_prev + bx, bx)
        a = jnp.where(use, a * a_prev, a)
        d *= 2
    a_sc[...] = a.reshape(tl, w)
    b_sc[...] = bx.reshape(tl, w)

    def body(kt, h_last):
        s = pl.multiple_of(kt * SUBLANES, SUBLANES)
        ht = a_sc[pl.ds(s, SUBLANES), :] * h_last + b_sc[pl.ds(s, SUBLANES), :]
        b_sc[pl.ds(s, SUBLANES), :] = ht
        return ht[SUBLANES - 1:, :]

    h_sc[...] = lax.fori_loop(0, tl // SUBLANES, body, h_sc[...])
    o_ref[0] = (b_sc[...] * _silu(g_ref[0].astype(F32))).astype(o_ref.dtype)


def _lru_gate_tiles(w_r, w_i):
    depth, nb = w_r.shape[:2]
    per = MXU_DIM // LRU_BLOCK
    nblk = nb // per

    def tile_diag(wg):
        wg = wg.astype(F32).reshape(depth, nblk, per, LRU_BLOCK, LRU_BLOCK)
        eye = jnp.eye(per, dtype=F32)
        return jnp.einsum('dkaio,ab->dkaibo', wg, eye).reshape(depth, nblk, MXU_DIM, MXU_DIM)

    return jnp.concatenate([tile_diag(w_r), tile_diag(w_i)], axis=3).astype(BF16)


def _lru(p3, x_blk, gate_blk, conv_w, conv_b, wg, b_r, b_i, lam, layer, tl):
    b, l, _ = p3.shape
    bw = conv_w.shape[1]
    nblk = bw // MXU_DIM
    vec = lambda a: a.reshape(1, bw).astype(F32)
    kern = functools.partial(_lru_kernel, tl=tl)
    full = lambda shape: pl.BlockSpec(shape, lambda bi, ti: (0,) * len(shape))
    return pl.pallas_call(
        kern,
        grid=(b, l // tl),
        in_specs=[pl.BlockSpec((1, tl, bw), lambda bi, ti: (bi, ti, x_blk)),
                  pl.BlockSpec((1, tl, bw), lambda bi, ti: (bi, ti, gate_blk)),
                  full((CONV_WIDTH, bw)), full((1, bw)),
                  pl.BlockSpec((None, nblk, MXU_DIM, 2 * MXU_DIM), lambda bi, ti: (layer, 0, 0, 0)),
                  full((1, bw)), full((1, bw)), full((1, bw))],
        out_specs=pl.BlockSpec((1, tl, bw), lambda bi, ti: (bi, ti, 0)),
        out_shape=jax.ShapeDtypeStruct((b, l, bw), BF16),
        scratch_shapes=[pltpu.VMEM((SUBLANES, bw), F32), pltpu.VMEM((1, bw), F32),
                        pltpu.VMEM((tl, bw), F32), pltpu.VMEM((tl, bw), F32)],
        compiler_params=_params("parallel", "arbitrary"), name="rg_lru",
    )(p3, p3, conv_w.astype(F32), vec(conv_b), wg, vec(b_r), vec(b_i), vec(lam))


def _tiles(b, l):
    t = b * l
    return dict(
        norm=min(256, t),
        mm_m=min(1024, t),
        mm_n=1024,
        cum=min(512, l),
        fox=min(512, l),
        sb_q=min(1024, l),
        sb_k=min(MXU_DIM, l),
        glu=min(512, t),
        lru=min(512, l),
    )


def kernel(x, g_pre, w_in, b_forget, ssm_a_re, ssm_a_im, ssm_log_dt, ssm_b_re, ssm_b_im, ssm_c_re, ssm_c_im, ssm_d, w_glu, b_glu, conv_w, conv_b, w_rgate, b_rgate, w_igate, b_igate, lru_lambda, w_out, g_post):
    b, l, d = x.shape
    depth = w_in.shape[0]
    bw = d // 4
    nh = bw // HEAD_DIM
    t = b * l
    ts = _tiles(b, l)
    assert w_in.shape[2] == 12 * bw + nh and l % SSM_CHUNK == 0

    def scaled_q(w, nblk, scale):
        sc = jnp.ones((nblk,), F32).at[0].set(scale)
        return (w.reshape(depth, d, nblk, bw) * sc[None, None, :, None]).reshape(depth, d, nblk * bw).astype(BF16)

    w_fox = scaled_q(w_in[:, :, :4 * bw], 4, HEAD_DIM ** -0.5 * LOG2E)
    w_rest = scaled_q(w_in[:, :, 4 * bw + nh:], 8, HEAD_DIM ** -0.5)
    w_f = jnp.pad(w_in[:, :, 4 * bw:4 * bw + nh], ((0, 0), (0, 0), (0, HEAD_DIM - nh))).astype(BF16)
    b_f = jnp.pad(b_forget.astype(F32), ((0, 0), (0, HEAD_DIM - nh))).reshape(depth, 1, HEAD_DIM)
    ssm_w = jax.vmap(_ssm_weights)(ssm_a_re, ssm_a_im, ssm_log_dt, ssm_b_re, ssm_b_im, ssm_c_re, ssm_c_im)
    lru_wg = _lru_gate_tiles(w_rgate, w_igate)
    w_glu_b = w_glu.astype(BF16)
    w_out_b = w_out.astype(BF16)

    h = x.reshape(t, d)
    u = _rmsnorm(h, g_pre[0], ts["norm"])
    for layer in range(depth):
        pf, log_f = _inproj(u, w_fox, 4 * bw, w_f, b_f, layer, ts["mm_m"], ts["mm_n"])
        pr = _inproj_plain(u, w_rest, layer, ts["mm_m"], ts["mm_n"])
        pf3 = pf.reshape(b, l, 4 * bw)
        pr3 = pr.reshape(b, l, 8 * bw)

        qx, kx = _forget_columns(log_f.reshape(b, l, HEAD_DIM), nh, ts["cum"])
        o_fox = _fox_attention(pf3, qx, kx, nh, ts["fox"])
        o_sb = _sb_attention(pr3, 0, nh, ts["sb_q"], ts["sb_k"])

        y_ssm = _ssm_scan(pr3, 4 * bw // SSM_UNIT, ssm_w, layer)
        o_ssm = _ssm_glu(y_ssm.reshape(t, bw), pr, 4, 5, ssm_d[layer], w_glu_b, b_glu[layer], layer, ts["glu"])

        o_lru = _lru(pr3, 6, 7, conv_w[layer], conv_b[layer], lru_wg, b_rgate[layer], b_igate[layer],
                     lru_lambda[layer], layer, ts["lru"])

        y = _outproj([o_fox.reshape(t, bw), o_sb.reshape(t, bw), o_ssm, o_lru.reshape(t, bw)],
                     w_out_b, layer, ts["mm_m"], ts["mm_n"])
        g_next = g_pre[layer + 1] if layer + 1 < depth else None
        h, u = _post(y, h, g_post[layer], g_next, ts["norm"])
    return h.reshape(b, l, d)
```

```python
import functools
import math

import jax
import jax.numpy as jnp
from jax import lax
from jax.experimental import pallas as pl
from jax.experimental.pallas import tpu as pltpu

F32 = jnp.float32
BF16 = jnp.bfloat16

HEAD_DIM = 128
SSM_GROUP = 16
SSM_STATE = 64
SSM_UNIT = 128
SSM_CHUNK = 8
LRU_BLOCK = 64
LRU_C = 8.0
CONV_WIDTH = 4
RMS_EPS = 1e-6
NEG_BIG = -1e30
LOG2E = math.log2(math.e)
TINY = 1e-37
SB_EXP_UNDERFLOW = 110.0
SUBLANES = 8
MXU_DIM = 256
VMEM_LIMIT = 56 * 1024 * 1024


def _params(*sem):
    return pltpu.CompilerParams(dimension_semantics=sem, vmem_limit_bytes=VMEM_LIMIT)


def _log_sigmoid(x):
    return jnp.minimum(x, 0.0) - jnp.log1p(jnp.exp(-jnp.abs(x)))


def _silu(x):
    return x * jax.nn.sigmoid(x)


def _rmsnorm_kernel(x_ref, g_ref, u_ref):
    x = x_ref[...]
    ms = jnp.mean(x * x, axis=-1, keepdims=True)
    u_ref[...] = (x * lax.rsqrt(ms + RMS_EPS) * g_ref[...]).astype(u_ref.dtype)


def _rmsnorm(x, g, tm):
    t, d = x.shape
    return pl.pallas_call(
        _rmsnorm_kernel,
        grid=(t // tm,),
        in_specs=[pl.BlockSpec((tm, d), lambda i: (i, 0)),
                  pl.BlockSpec((1, d), lambda i: (0, 0))],
        out_specs=pl.BlockSpec((tm, d), lambda i: (i, 0)),
        out_shape=jax.ShapeDtypeStruct((t, d), BF16),
        compiler_params=_params("parallel"), name="rmsnorm_first",
    )(x, g.reshape(1, d))


def _post_kernel(y_ref, h_ref, gpost_ref, gnext_ref, hn_ref, u_ref):
    y = y_ref[...].astype(F32)
    ms = jnp.mean(y * y, axis=-1, keepdims=True)
    hn = h_ref[...] + y * lax.rsqrt(ms + RMS_EPS) * gpost_ref[...]
    hn_ref[...] = hn
    ms2 = jnp.mean(hn * hn, axis=-1, keepdims=True)
    u_ref[...] = (hn * lax.rsqrt(ms2 + RMS_EPS) * gnext_ref[...]).astype(u_ref.dtype)


def _post_last_kernel(y_ref, h_ref, gpost_ref, hn_ref):
    y = y_ref[...].astype(F32)
    ms = jnp.mean(y * y, axis=-1, keepdims=True)
    hn_ref[...] = h_ref[...] + y * lax.rsqrt(ms + RMS_EPS) * gpost_ref[...]


def _post(y, h, g_post, g_next, tm):
    t, d = y.shape
    row = pl.BlockSpec((tm, d), lambda i: (i, 0))
    vec = pl.BlockSpec((1, d), lambda i: (0, 0))
    if g_next is None:
        return pl.pallas_call(
            _post_last_kernel, grid=(t // tm,),
            in_specs=[row, row, vec], out_specs=row,
            out_shape=jax.ShapeDtypeStruct((t, d), F32),
            compiler_params=_params("parallel"), name="post_norm_last",
        )(y, h, g_post.reshape(1, d)), None
    return pl.pallas_call(
        _post_kernel, grid=(t // tm,),
        in_specs=[row, row, vec, vec], out_specs=[row, row],
        out_shape=[jax.ShapeDtypeStruct((t, d), F32), jax.ShapeDtypeStruct((t, d), BF16)],
        compiler_params=_params("parallel"), name="post_norm",
    )(y, h, g_post.reshape(1, d), g_next.reshape(1, d))


def _inproj_kernel(u_ref, w_ref, wf_ref, bf_ref, p_ref, lf_ref):
    u = u_ref[...]
    p_ref[...] = jnp.dot(u, w_ref[...], preferred_element_type=F32).astype(p_ref.dtype)

    @pl.when(pl.program_id(1) == 0)
    def _():
        logit = jnp.dot(u, wf_ref[...], preferred_element_type=F32) + bf_ref[...]
        lf_ref[...] = _log_sigmoid(logit)


def _matmul_kernel(u_ref, w_ref, p_ref):
    p_ref[...] = jnp.dot(u_ref[...], w_ref[...], preferred_element_type=F32).astype(p_ref.dtype)


def _inproj_plain(u, w, layer, tm, tn):
    t, d = u.shape
    n = w.shape[2]
    return pl.pallas_call(
        _matmul_kernel,
        grid=(t // tm, n // tn),
        in_specs=[pl.BlockSpec((tm, d), lambda i, j: (i, 0)),
                  pl.BlockSpec((None, d, tn), lambda i, j: (layer, 0, j))],
        out_specs=pl.BlockSpec((tm, tn), lambda i, j: (i, j)),
        out_shape=jax.ShapeDtypeStruct((t, n), BF16),
        compiler_params=_params("parallel", "arbitrary"), name="in_proj_rest",
    )(u, w)


def _inproj(u, w, n, wf, bf, layer, tm, tn):
    t, d = u.shape
    return pl.pallas_call(
        _inproj_kernel,
        grid=(t // tm, n // tn),
        in_specs=[pl.BlockSpec((tm, d), lambda i, j: (i, 0)),
                  pl.BlockSpec((None, d, tn), lambda i, j: (layer, 0, j)),
                  pl.BlockSpec((None, d, HEAD_DIM), lambda i, j: (layer, 0, 0)),
                  pl.BlockSpec((None, 1, HEAD_DIM), lambda i, j: (layer, 0, 0))],
        out_specs=[pl.BlockSpec((tm, tn), lambda i, j: (i, j)),
                   pl.BlockSpec((tm, HEAD_DIM), lambda i, j: (i, 0))],
        out_shape=[jax.ShapeDtypeStruct((t, n), BF16),
                   jax.ShapeDtypeStruct((t, HEAD_DIM), F32)],
        compiler_params=_params("parallel", "arbitrary"), name="in_proj",
    )(u, w, wf, bf)


def _outproj_kernel(a_ref, b_ref, c_ref, d_ref, w_ref, y_ref):
    x = jnp.concatenate([a_ref[...], b_ref[...], c_ref[...], d_ref[...]], axis=1)
    y_ref[...] = jnp.dot(x, w_ref[...], preferred_element_type=F32).astype(y_ref.dtype)


def _outproj(branches, w, layer, tm, tn):
    t, bw = branches[0].shape
    _, k, n = w.shape
    bspec = pl.BlockSpec((tm, bw), lambda i, j: (i, 0))
    return pl.pallas_call(
        _outproj_kernel,
        grid=(t // tm, n // tn),
        in_specs=[bspec, bspec, bspec, bspec, pl.BlockSpec((None, k, tn), lambda i, j: (layer, 0, j))],
        out_specs=pl.BlockSpec((tm, tn), lambda i, j: (i, j)),
        out_shape=jax.ShapeDtypeStruct((t, n), BF16),
        compiler_params=_params("parallel", "arbitrary"), name="out_proj",
    )(*branches, w)


def _split3(x):
    hi = x.astype(BF16).astype(F32)
    r1 = x - hi
    mid = r1.astype(BF16).astype(F32)
    lo = (r1 - mid).astype(BF16).astype(F32)
    return hi, mid, lo


def _cumsum_kernel(lf_ref, tri_ref, qx_ref, kx_ref, carry_ref, *, n_heads):
    @pl.when(pl.program_id(1) == 0)
    def _():
        carry_ref[...] = jnp.zeros_like(carry_ref)

    hi, mid, lo = _split3(lf_ref[0])
    tri = tri_ref[...]
    c = (jnp.dot(tri, hi.astype(BF16), preferred_element_type=F32)
         + jnp.dot(tri, mid.astype(BF16), preferred_element_type=F32)
         + jnp.dot(tri, lo.astype(BF16), preferred_element_type=F32)) + carry_ref[...]
    carry_ref[...] = c[-1:, :]

    c2 = c * LOG2E
    lane = lax.broadcasted_iota(jnp.int32, c2.shape, 1)
    ones = jnp.where(lane < 3, 1.0, 0.0)
    for h in range(n_heads):
        f = jnp.sum(jnp.where(lane == h, c2, 0.0), axis=1, keepdims=True)
        hi, mid, lo = _split3(f)
        pieces = jnp.where(lane == 0, hi, jnp.where(lane == 1, mid, jnp.where(lane == 2, lo, 0.0)))
        qx_ref[0, h] = (pieces + pltpu.roll(ones, 3, 1)).astype(qx_ref.dtype)
        kx_ref[0, h] = (ones - pltpu.roll(pieces, 3, 1)).astype(kx_ref.dtype)


def _forget_columns(lf, n_heads, tc):
    b, l, w = lf.shape
    tri = jnp.tril(jnp.ones((tc, tc), F32)).astype(BF16)
    out = jax.ShapeDtypeStruct((b, n_heads, l, w), BF16)
    ospec = pl.BlockSpec((1, n_heads, tc, w), lambda i, j: (i, 0, j, 0))
    return pl.pallas_call(
        functools.partial(_cumsum_kernel, n_heads=n_heads),
        grid=(b, l // tc),
        in_specs=[pl.BlockSpec((1, tc, w), lambda i, j: (i, j, 0)),
                  pl.BlockSpec((tc, tc), lambda i, j: (0, 0))],
        out_specs=[ospec, ospec],
        out_shape=[out, out],
        scratch_shapes=[pltpu.VMEM((1, w), F32)],
        compiler_params=_params("parallel", "arbitrary"), name="forget_cumsum",
    )(lf, tri)


def _fox_kernel(q_ref, qx_ref, k_ref, kx_ref, v_ref, g_ref, o_ref, s_sc, m_sc, acc_sc, *, tq, nq):
    acc_sc[...] = jnp.zeros_like(acc_sc)
    nrep = tq // HEAD_DIM
    ones = jnp.ones((tq, HEAD_DIM), BF16)
    contract_last = (((1,), (1,)), ((), ()))
    ahead = lax.broadcasted_iota(jnp.int32, (tq, tq), 1) - lax.broadcasted_iota(jnp.int32, (tq, tq), 0)

    def rows_of(blk):
        return pl.ds(pl.multiple_of(blk * tq, tq), tq)

    def scores(qi, kb, slot):
        q = jnp.concatenate([q_ref[0, rows_of(qi), :], qx_ref[0, 0, rows_of(qi), :]], axis=1)
        k = jnp.concatenate([k_ref[0, rows_of(kb), :], kx_ref[0, 0, rows_of(kb), :]], axis=1)
        s_sc[slot] = lax.dot_general(q, k, contract_last, preferred_element_type=F32)

    def accumulate(qi, kb, slot):
        v = jnp.concatenate([v_ref[0, rows_of(kb), :], ones], axis=1)
        s = jnp.where(ahead <= (qi - kb) * tq, s_sc[slot], NEG_BIG)
        m_prev = jnp.where(kb == 0, NEG_BIG, m_sc[...])
        m_new = jnp.maximum(m_prev, jnp.max(s, axis=1, keepdims=True))
        alpha = jnp.exp2(m_prev - m_new)
        p = jnp.exp2(s - jnp.tile(m_new, (1, nrep)))
        acc_sc[rows_of(qi), :] = (jnp.tile(alpha, (1, 2)) * acc_sc[rows_of(qi), :]
                                  + jnp.dot(p.astype(BF16), v, preferred_element_type=F32))
        m_sc[...] = m_new

    def step(tile, slot, prefetch):
        qi, kb = tile
        last = kb == qi
        nxt = (jnp.where(last, qi + 1, qi), jnp.where(last, 0, kb + 1))
        if prefetch:
            scores(jnp.minimum(nxt[0], nq - 1), nxt[1], 1 - slot)
        accumulate(qi, kb, slot)
        return nxt

    n_tiles = nq * (nq + 1) // 2
    unroll = 8
    scores(0, 0, 0)

    def body(i, tile):
        for j in range(unroll):
            tile = step(tile, j % 2, True)
        return tile

    tile = lax.fori_loop(0, n_tiles // unroll, body, (jnp.int32(0), jnp.int32(0)))
    left = n_tiles % unroll
    for j in range(left):
        tile = step(tile, j % 2, j + 1 < left)

    def finish(blk, carry):
        acc = acc_sc[rows_of(blk), :]
        gate = g_ref[0, rows_of(blk), :].astype(F32)
        o_ref[0, rows_of(blk), :] = (acc[:, :HEAD_DIM] / acc[:, HEAD_DIM:] * _silu(gate)).astype(o_ref.dtype)
        return carry

    lax.fori_loop(0, nq, finish, 0)


def _fox_attention(p3, qx, kx, n_heads, tq):
    b, l, _ = p3.shape
    nq = l // tq
    kern = functools.partial(_fox_kernel, tq=tq, nq=nq)
    col = lambda blk0: pl.BlockSpec((1, l, HEAD_DIM), lambda bi, h: (bi, 0, blk0 + h))
    per_head = pl.BlockSpec((1, 1, l, HEAD_DIM), lambda bi, h: (bi, h, 0, 0))
    return pl.pallas_call(
        kern,
        grid=(b, n_heads),
        in_specs=[col(0), per_head, col(n_heads), per_head, col(2 * n_heads), col(3 * n_heads)],
        out_specs=col(0),
        out_shape=jax.ShapeDtypeStruct((b, l, n_heads * HEAD_DIM), BF16),
        scratch_shapes=[pltpu.VMEM((2, tq, tq), F32), pltpu.VMEM((tq, HEAD_DIM), F32),
                        pltpu.VMEM((l, 2 * HEAD_DIM), F32)],
        compiler_params=_params("parallel", "arbitrary"), name="fox_attn",
    )(p3, qx, p3, kx, p3, p3)


def _sb_kernel(q_ref, k_ref, v_ref, g_ref, tri_ref, o_ref, z_sc, cost_sc, acc_sc, *, tq, tk):
    qi = pl.program_id(2)
    ng = tq // tk
    cost_sc[...] = jnp.zeros_like(cost_sc)
    acc_sc[...] = jnp.zeros_like(acc_sc)
    groups = [pl.ds(r * tk, tk) for r in range(ng)]
    last_step = qi * ng + ng - 1

    def key_rows(r, n):
        kb = jnp.maximum(qi * ng + r - n, 0)
        return pl.ds(pl.multiple_of(kb * tk, tk), tk)

    def logits(n, slot):
        for r, rows in enumerate(groups):
            z_sc[slot, rows, :] = lax.dot_general(q_ref[0, rows, :], k_ref[0, key_rows(r, n), :],
                                                  (((1,), (1,)), ((), ())), preferred_element_type=F32)

    def process(n, slot, diagonal):
        for r, rows in enumerate(groups):
            z = z_sc[slot, rows, :]
            softplus = jnp.maximum(z, 0.0) + jnp.log(1.0 + jnp.exp(-jnp.abs(z)))
            log_beta = z - softplus
            if diagonal:
                mask = (lax.broadcasted_iota(jnp.int32, (tk, tk), 1)
                        < lax.broadcasted_iota(jnp.int32, (tk, tk), 0))
                softplus = jnp.where(mask, softplus, 0.0)
            within = jnp.dot(softplus.astype(BF16), tri_ref[...], preferred_element_type=F32)
            w = jnp.exp(log_beta - within - jnp.tile(cost_sc[rows, :], (1, tk // HEAD_DIM)))
            if diagonal:
                w = jnp.where(mask, w, 0.0)
            pv = jnp.dot(w.astype(BF16), v_ref[0, key_rows(r, n), :], preferred_element_type=F32)
            spent = jnp.sum(softplus, axis=1, keepdims=True)
            if not diagonal:
                has_keys = (qi * ng + r - n >= 0).astype(F32)
                pv = pv * has_keys
                spent = spent * has_keys
            acc_sc[rows, :] += pv
            cost_sc[rows, :] += spent

    logits(0, 0)
    logits(1, 1)
    process(0, 0, True)
    logits(2, 0)
    process(1, 1, False)

    def cond(state):
        return jnp.logical_and(state[0] <= last_step, state[1] < SB_EXP_UNDERFLOW)

    def body(state):
        n = state[0]
        logits(n + 1, 1)
        process(n, 0, False)
        logits(n + 2, 0)
        process(n + 1, 1, False)
        return n + 2, jnp.min(cost_sc[...])

    lax.while_loop(cond, body, (2, jnp.min(cost_sc[...])))
    gate = g_ref[0].astype(F32)
    o_ref[0] = (acc_sc[...] * _silu(gate)).astype(o_ref.dtype)


def _sb_attention(p3, col0, n_heads, tq, tk):
    b, l, _ = p3.shape
    nq = l // tq
    tri = jnp.tril(jnp.ones((tk, tk), F32), k=-1).astype(BF16)
    kern = functools.partial(_sb_kernel, tq=tq, tk=tk)
    return pl.pallas_call(
        kern,
        grid=(b, n_heads, nq),
        in_specs=[pl.BlockSpec((1, tq, HEAD_DIM), lambda bi, h, qi: (bi, qi, col0 + h)),
                  pl.BlockSpec((1, l, HEAD_DIM), lambda bi, h, qi: (bi, 0, col0 + n_heads + h)),
                  pl.BlockSpec((1, l, HEAD_DIM), lambda bi, h, qi: (bi, 0, col0 + 2 * n_heads + h)),
                  pl.BlockSpec((1, tq, HEAD_DIM), lambda bi, h, qi: (bi, qi, col0 + 3 * n_heads + h)),
                  pl.BlockSpec((tk, tk), lambda bi, h, qi: (0, 0))],
        out_specs=pl.BlockSpec((1, tq, HEAD_DIM), lambda bi, h, qi: (bi, qi, h)),
        out_shape=jax.ShapeDtypeStruct((b, l, n_heads * HEAD_DIM), BF16),
        scratch_shapes=[pltpu.VMEM((2, tq, tk), F32), pltpu.VMEM((tq, HEAD_DIM), F32),
                        pltpu.VMEM((tq, HEAD_DIM), F32)],
        compiler_params=_params("parallel", "parallel", "arbitrary"), name="sb_attn",
    )(p3, p3, p3, p3, tri)


def _ssm_weights(a_re, a_im, log_dt, b_re, b_im, c_re, c_im):
    g, p = a_re.shape
    hg = b_re.shape[-1]
    c = SSM_CHUNK
    dt = jnp.exp(log_dt.astype(F32))[:, None]
    a_re = a_re.astype(F32)
    a_im = a_im.astype(F32)
    n = jnp.arange(c + 1, dtype=F32)[:, None, None]
    mag = jnp.exp(a_re * dt * n)
    ang = a_im * dt * n
    pw_re, pw_im = mag * jnp.cos(ang), mag * jnp.sin(ang)
    num_re, num_im = pw_re[1] - 1.0, pw_im[1]
    den = a_re * a_re + a_im * a_im
    f_re = (num_re * a_re + num_im * a_im) / den
    f_im = (num_im * a_re - num_re * a_im) / den
    bb_re = f_re[..., None] * b_re - f_im[..., None] * b_im
    bb_im = f_re[..., None] * b_im + f_im[..., None] * b_re
    cc_re, cc_im = c_re.astype(F32), c_im.astype(F32)
    hi = lax.Precision.HIGHEST
    cl_re = cc_re[None] * pw_re[:, :, None, :] - cc_im[None] * pw_im[:, :, None, :]
    cl_im = cc_re[None] * pw_im[:, :, None, :] + cc_im[None] * pw_re[:, :, None, :]
    cl = jnp.concatenate([cl_re[:c], cl_im[:c]], axis=3).transpose(1, 0, 2, 3)
    kk = jnp.matmul(cl.reshape(g, c * hg, 2 * p), jnp.concatenate([bb_re, -bb_im], axis=1), precision=hi)
    kk = kk.reshape(g, c, hg, hg).transpose(1, 0, 2, 3)
    ug = SSM_UNIT // hg
    nu = g // ug

    def spread(x, inner):
        rows, w = x.shape
        y = jnp.dot(x.astype(BF16), jnp.tile(jnp.eye(w, dtype=BF16), (1, ug)), preferred_element_type=F32)
        row_group = (lax.broadcasted_iota(jnp.int32, y.shape, 0) // inner) % ug
        col_group = lax.broadcasted_iota(jnp.int32, y.shape, 1) // w
        return jnp.where(row_group == col_group, y, 0.0).astype(BF16)

    bd = spread(kk.transpose(0, 1, 3, 2).reshape(c * g * hg, hg), hg).reshape(c, nu, SSM_UNIT, SSM_UNIT)
    zero = jnp.zeros_like(bd[0])
    m = jnp.concatenate([jnp.concatenate([bd[t - s] if t >= s else zero for t in range(c)], axis=2)
                         for s in range(c)], axis=1)
    n_rev = jnp.arange(c - 1, -1, -1, dtype=F32)[:, None, None]
    pr = jnp.exp(a_re * dt * n_rev) * jnp.cos(a_im * dt * n_rev)
    pi_ = jnp.exp(a_re * dt * n_rev) * jnp.sin(a_im * dt * n_rev)
    be_re = pr[..., None] * bb_re[None] - pi_[..., None] * bb_im[None]
    be_im = pr[..., None] * bb_im[None] + pi_[..., None] * bb_re[None]

    def bend(x):
        x = x.reshape(c, nu, ug, p, hg).transpose(1, 0, 2, 4, 3)
        return spread(x.reshape(nu * c * SSM_UNIT, p), hg).reshape(nu, c * SSM_UNIT, ug * p)

    def cout(x):
        x = x.reshape(c, nu, ug, hg, p).transpose(0, 1, 2, 4, 3)
        x = spread(x.reshape(c * nu * ug * p, hg), p).reshape(c, nu, ug * p, SSM_UNIT)
        return jnp.concatenate([x[t] for t in range(c)], axis=2)

    lam_re = pw_re[c].reshape(nu, 1, ug * p)
    lam_im = pw_im[c].reshape(nu, 1, ug * p)
    return m, bend(be_re), bend(be_im), cout(cl_re[1:]), cout(-cl_im[1:]), lam_re, lam_im


def _ssm_kernel(x_ref, m_ref, bre_ref, bim_ref, cre_ref, cim_ref, lre_ref, lim_ref, y_ref,
                xf_sc, zre_sc, zim_sc, yf_sc, *, n_chunk):
    c = SSM_CHUNK
    xf_sc[...] = x_ref[0].astype(F32)
    u = jnp.concatenate([xf_sc[pl.ds(s, n_chunk, stride=c), :] for s in range(c)], axis=1).astype(BF16)
    zre_sc[...] = jnp.dot(u, bre_ref[0], preferred_element_type=F32)
    zim_sc[...] = jnp.dot(u, bim_ref[0], preferred_element_type=F32)
    lre = lre_ref[0]
    lim = lim_ref[0]

    def body(kc, carry):
        xr, xi = carry
        zr = zre_sc[pl.ds(kc, 1), :]
        zi = zim_sc[pl.ds(kc, 1), :]
        zre_sc[pl.ds(kc, 1), :] = xr
        zim_sc[pl.ds(kc, 1), :] = xi
        return lre * xr - lim * xi + zr, lre * xi + lim * xr + zi

    zero = jnp.zeros((1, zre_sc.shape[1]), F32)
    lax.fori_loop(0, n_chunk, body, (zero, zero))
    y = (jnp.dot(u, m_ref[0], preferred_element_type=F32)
         + jnp.dot(zre_sc[...].astype(BF16), cre_ref[0], preferred_element_type=F32)
         + jnp.dot(zim_sc[...].astype(BF16), cim_ref[0], preferred_element_type=F32))
    for t in range(c):
        yf_sc[pl.ds(t, n_chunk, stride=c), :] = y[:, t * SSM_UNIT:(t + 1) * SSM_UNIT]
    y_ref[0] = yf_sc[...].astype(y_ref.dtype)


def _ssm_scan(p3, x_blk0, weights, layer):
    b, l, _ = p3.shape
    m, bre, bim, cre, cim, lre, lim = weights
    nu = m.shape[1]
    lanes = m.shape[2]
    sp = bre.shape[3]
    nck = l // SSM_CHUNK
    kern = functools.partial(_ssm_kernel, n_chunk=nck)
    per_unit = lambda r, cc: pl.BlockSpec((None, 1, r, cc), lambda u, bi: (layer, u, 0, 0))
    return pl.pallas_call(
        kern,
        grid=(nu, b),
        in_specs=[pl.BlockSpec((1, l, SSM_UNIT), lambda u, bi: (bi, 0, x_blk0 + u)),
                  per_unit(lanes, lanes), per_unit(lanes, sp), per_unit(lanes, sp),
                  per_unit(sp, lanes), per_unit(sp, lanes), per_unit(1, sp), per_unit(1, sp)],
        out_specs=pl.BlockSpec((1, l, SSM_UNIT), lambda u, bi: (bi, 0, u)),
        out_shape=jax.ShapeDtypeStruct((b, l, nu * SSM_UNIT), BF16),
        scratch_shapes=[pltpu.VMEM((l, SSM_UNIT), F32), pltpu.VMEM((nck, sp), F32),
                        pltpu.VMEM((nck, sp), F32), pltpu.VMEM((l, SSM_UNIT), F32)],
        compiler_params=_params("parallel", "arbitrary"), name="ssm_scan",
    )(p3, m, bre, bim, cre, cim, lre, lim)


def _glu_kernel(y_ref, u_ref, g_ref, d_ref, w_ref, bg_ref, o_ref):
    y = y_ref[...].astype(F32) + d_ref[...] * u_ref[...].astype(F32)
    y = jax.nn.gelu(y)
    z = jnp.dot(y.astype(BF16), w_ref[...], preferred_element_type=F32) + bg_ref[...]
    o_ref[...] = (y * jax.nn.sigmoid(z) * _silu(g_ref[...].astype(F32))).astype(o_ref.dtype)


def _ssm_glu(y, p2, u_blk, gate_blk, d_skip, w_glu, b_glu, layer, tm):
    t, bw = y.shape
    return pl.pallas_call(
        _glu_kernel,
        grid=(t // tm,),
        in_specs=[pl.BlockSpec((tm, bw), lambda i: (i, 0)),
                  pl.BlockSpec((tm, bw), lambda i: (i, u_blk)),
                  pl.BlockSpec((tm, bw), lambda i: (i, gate_blk)),
                  pl.BlockSpec((1, bw), lambda i: (0, 0)),
                  pl.BlockSpec((None, bw, bw), lambda i: (layer, 0, 0)),
                  pl.BlockSpec((1, bw), lambda i: (0, 0))],
        out_specs=pl.BlockSpec((tm, bw), lambda i: (i, 0)),
        out_shape=jax.ShapeDtypeStruct((t, bw), BF16),
        compiler_params=_params("parallel"), name="ssm_glu",
    )(y, p2, p2, d_skip.reshape(1, bw).astype(F32), w_glu, b_glu.reshape(1, bw).astype(F32))


def _lru_kernel(x_ref, g_ref, cw_ref, cb_ref, wg_ref, br_ref, bi_ref, lam_ref, o_ref,
                tail_sc, h_sc, a_sc, b_sc, *, tl):
    @pl.when(pl.program_id(1) == 0)
    def _():
        tail_sc[...] = jnp.zeros_like(tail_sc)
        h_sc[...] = jnp.zeros_like(h_sc)

    x = x_ref[0].astype(F32)
    w = x.shape[1]
    xcat = jnp.concatenate([tail_sc[...], x], axis=0)
    xc = cb_ref[...] + cw_ref[CONV_WIDTH - 1:CONV_WIDTH, :] * x
    for d in range(1, CONV_WIDTH):
        xd = pltpu.roll(xcat, d, 0)[SUBLANES:, :]
        xc = xc + cw_ref[CONV_WIDTH - 1 - d:CONV_WIDTH - d, :] * xd
    tail_sc[...] = x[tl - SUBLANES:, :]

    xb = xc.astype(BF16)
    nblk = w // MXU_DIM
    gates = [jnp.dot(xb[:, kb * MXU_DIM:(kb + 1) * MXU_DIM], wg_ref[kb], preferred_element_type=F32)
             for kb in range(nblk)]
    gr = jnp.concatenate([gt[:, :MXU_DIM] for gt in gates], axis=1)
    gi = jnp.concatenate([gt[:, MXU_DIM:] for gt in gates], axis=1)
    r = jax.nn.sigmoid(gr + br_ref[...])
    i = jax.nn.sigmoid(gi + bi_ref[...])
    log_a = r * (LRU_C * _log_sigmoid(lam_ref[...]))
    a = jnp.exp(log_a)
    gap = 1.0 - a * a
    bx = gap * lax.rsqrt(jnp.maximum(gap, TINY)) * (i * xc)

    nt = tl // SUBLANES
    a = a.reshape(nt, SUBLANES, w)
    bx = bx.reshape(nt, SUBLANES, w)
    sub = lax.broadcasted_iota(jnp.int32, (nt, SUBLANES, w), 1)
    d = 1
    while d < SUBLANES:
        a_prev = pltpu.roll(a, d, 1)
        b_prev = pltpu.roll(bx, d, 1)
        use = sub >= d
        bx = jnp.where(use, a * b_prev + bx, bx)
        a = jnp.where(use, a * a_prev, a)
        d *= 2
    a_sc[...] = a.reshape(tl, w)
    b_sc[...] = bx.reshape(tl, w)

    def body(kt, h_last):
        s = pl.multiple_of(kt * SUBLANES, SUBLANES)
        ht = a_sc[pl.ds(s, SUBLANES), :] * h_last + b_sc[pl.ds(s, SUBLANES), :]
        b_sc[pl.ds(s, SUBLANES), :] = ht
        return ht[SUBLANES - 1:, :]

    h_sc[...] = lax.fori_loop(0, tl // SUBLANES, body, h_sc[...])
    o_ref[0] = (b_sc[...] * _silu(g_ref[0].astype(F32))).astype(o_ref.dtype)


def _lru_gate_tiles(w_r, w_i):
    depth, nb = w_r.shape[:2]
    per = MXU_DIM // LRU_BLOCK
    nblk = nb // per

    def tile_diag(wg):
        wg = wg.astype(F32).reshape(depth, nblk, per, LRU_BLOCK, LRU_BLOCK)
        eye = jnp.eye(per, dtype=F32)
        return jnp.einsum('dkaio,ab->dkaibo', wg, eye).reshape(depth, nblk, MXU_DIM, MXU_DIM)

    return jnp.concatenate([tile_diag(w_r), tile_diag(w_i)], axis=3).astype(BF16)


def _lru(p3, x_blk, gate_blk, conv_w, conv_b, wg, b_r, b_i, lam, layer, tl):
    b, l, _ = p3.shape
    bw = conv_w.shape[1]
    nblk = bw // MXU_DIM
    vec = lambda a: a.reshape(1, bw).astype(F32)
    kern = functools.partial(_lru_kernel, tl=tl)
    full = lambda shape: pl.BlockSpec(shape, lambda bi, ti: (0,) * len(shape))
    return pl.pallas_call(
        kern,
        grid=(b, l // tl),
        in_specs=[pl.BlockSpec((1, tl, bw), lambda bi, ti: (bi, ti, x_blk)),
                  pl.BlockSpec((1, tl, bw), lambda bi, ti: (bi, ti, gate_blk)),
                  full((CONV_WIDTH, bw)), full((1, bw)),
                  pl.BlockSpec((None, nblk, MXU_DIM, 2 * MXU_DIM), lambda bi, ti: (layer, 0, 0, 0)),
                  full((1, bw)), full((1, bw)), full((1, bw))],
        out_specs=pl.BlockSpec((1, tl, bw), lambda bi, ti: (bi, ti, 0)),
        out_shape=jax.ShapeDtypeStruct((b, l, bw), BF16),
        scratch_shapes=[pltpu.VMEM((SUBLANES, bw), F32), pltpu.VMEM((1, bw), F32),
                        pltpu.VMEM((tl, bw), F32), pltpu.VMEM((tl, bw), F32)],
        compiler_params=_params("parallel", "arbitrary"), name="rg_lru",
    )(p3, p3, conv_w.astype(F32), vec(conv_b), wg, vec(b_r), vec(b_i), vec(lam))


def _tiles(b, l):
    t = b * l
    return dict(
        norm=min(256, t),
        mm_m=min(1024, t),
        mm_n=1024,
        cum=min(512, l),
        fox=min(512, l),
        sb_q=min(1024, l),
        sb_k=min(MXU_DIM, l),
        glu=min(512, t),
        lru=min(512, l),
    )


def kernel(x, g_pre, w_in, b_forget, ssm_a_re, ssm_a_im, ssm_log_dt, ssm_b_re, ssm_b_im, ssm_c_re, ssm_c_im, ssm_d, w_glu, b_glu, conv_w, conv_b, w_rgate, b_rgate, w_igate, b_igate, lru_lambda, w_out, g_post):
    b, l, d = x.shape
    depth = w_in.shape[0]
    bw = d // 4
    nh = bw // HEAD_DIM
    t = b * l
    ts = _tiles(b, l)
    assert w_in.shape[2] == 12 * bw + nh and l % SSM_CHUNK == 0

    def scaled_q(w, nblk, scale):
        sc = jnp.ones((nblk,), F32).at[0].set(scale)
        return (w.reshape(depth, d, nblk, bw) * sc[None, None, :, None]).reshape(depth, d, nblk * bw).astype(BF16)

    w_fox = scaled_q(w_in[:, :, :4 * bw], 4, HEAD_DIM ** -0.5 * LOG2E)
    w_rest = scaled_q(w_in[:, :, 4 * bw + nh:], 8, HEAD_DIM ** -0.5)
    w_f = jnp.pad(w_in[:, :, 4 * bw:4 * bw + nh], ((0, 0), (0, 0), (0, HEAD_DIM - nh))).astype(BF16)
    b_f = jnp.pad(b_forget.astype(F32), ((0, 0), (0, HEAD_DIM - nh))).reshape(depth, 1, HEAD_DIM)
    ssm_w = jax.vmap(_ssm_weights)(ssm_a_re, ssm_a_im, ssm_log_dt, ssm_b_re, ssm_b_im, ssm_c_re, ssm_c_im)
    lru_wg = _lru_gate_tiles(w_rgate, w_igate)
    w_glu_b = w_glu.astype(BF16)
    w_out_b = w_out.astype(BF16)

    h = x.reshape(t, d)
    u = _rmsnorm(h, g_pre[0], ts["norm"])
    for layer in range(depth):
        pf, log_f = _inproj(u, w_fox, 4 * bw, w_f, b_f, layer, ts["mm_m"], ts["mm_n"])
        pr = _inproj_plain(u, w_rest, layer, ts["mm_m"], ts["mm_n"])
        pf3 = pf.reshape(b, l, 4 * bw)
        pr3 = pr.reshape(b, l, 8 * bw)

        qx, kx = _forget_columns(log_f.reshape(b, l, HEAD_DIM), nh, ts["cum"])
        o_fox = _fox_attention(pf3, qx, kx, nh, ts["fox"])
        o_sb = _sb_attention(pr3, 0, nh, ts["sb_q"], ts["sb_k"])

        y_ssm = _ssm_scan(pr3, 4 * bw // SSM_UNIT, ssm_w, layer)
        o_ssm = _ssm_glu(y_ssm.reshape(t, bw), pr, 4, 5, ssm_d[layer], w_glu_b, b_glu[layer], layer, ts["glu"])

        o_lru = _lru(pr3, 6, 7, conv_w[layer], conv_b[layer], lru_wg, b_rgate[layer], b_igate[layer],
                     lru_lambda[layer], layer, ts["lru"])

        y = _outproj([o_fox.reshape(t, bw), o_sb.reshape(t, bw), o_ssm, o_lru.reshape(t, bw)],
                     w_out_b, layer, ts["mm_m"], ts["mm_n"])
        g_next = g_pre[layer + 1] if layer + 1 < depth else None
        h, u = _post(y, h, g_post[layer], g_next, ts["norm"])
    return h.reshape(b, l, d)
```

```python
import functools
import math

import jax
import jax.numpy as jnp
from jax import lax
from jax.experimental import pallas as pl
from jax.experimental.pallas import tpu as pltpu

F32 = jnp.float32
BF16 = jnp.bfloat16

HEAD_DIM = 128
SSM_UNIT = 128
SSM_CHUNK = 8
LRU_BLOCK = 64
LRU_C = 8.0
CONV_WIDTH = 4
RMS_EPS = 1e-6
NEG_BIG = -1e30
LOG2E = math.log2(math.e)
TINY = 1e-37
SB_EXP_UNDERFLOW = 110.0
SUBLANES = 8
MXU_DIM = 256
VMEM_LIMIT = 56 * 1024 * 1024


def _params(*sem):
    return pltpu.CompilerParams(dimension_semantics=sem, vmem_limit_bytes=VMEM_LIMIT)


def _log_sigmoid(x):
    return jnp.minimum(x, 0.0) - jnp.log1p(jnp.exp(-jnp.abs(x)))


def _silu(x):
    return x * jax.nn.sigmoid(x)


def _rmsnorm_kernel(x_ref, g_ref, u_ref):
    x = x_ref[...]
    ms = jnp.mean(x * x, axis=-1, keepdims=True)
    u_ref[...] = (x * lax.rsqrt(ms + RMS_EPS) * g_ref[...]).astype(u_ref.dtype)


def _rmsnorm(x, g, tm):
    t, d = x.shape
    return pl.pallas_call(
        _rmsnorm_kernel,
        grid=(t // tm,),
        in_specs=[pl.BlockSpec((tm, d), lambda i: (i, 0)),
                  pl.BlockSpec((1, d), lambda i: (0, 0))],
        out_specs=pl.BlockSpec((tm, d), lambda i: (i, 0)),
        out_shape=jax.ShapeDtypeStruct((t, d), BF16),
        compiler_params=_params("parallel"), name="rmsnorm_first",
    )(x, g.reshape(1, d))


def _post_kernel(y_ref, h_ref, gpost_ref, gnext_ref, hn_ref, u_ref):
    y = y_ref[...].astype(F32)
    ms = jnp.mean(y * y, axis=-1, keepdims=True)
    hn = h_ref[...] + y * lax.rsqrt(ms + RMS_EPS) * gpost_ref[...]
    hn_ref[...] = hn
    ms2 = jnp.mean(hn * hn, axis=-1, keepdims=True)
    u_ref[...] = (hn * lax.rsqrt(ms2 + RMS_EPS) * gnext_ref[...]).astype(u_ref.dtype)


def _post_last_kernel(y_ref, h_ref, gpost_ref, hn_ref):
    y = y_ref[...].astype(F32)
    ms = jnp.mean(y * y, axis=-1, keepdims=True)
    hn_ref[...] = h_ref[...] + y * lax.rsqrt(ms + RMS_EPS) * gpost_ref[...]


def _post(y, h, g_post, g_next, tm):
    t, d = y.shape
    row = pl.BlockSpec((tm, d), lambda i: (i, 0))
    vec = pl.BlockSpec((1, d), lambda i: (0, 0))
    if g_next is None:
        return pl.pallas_call(
            _post_last_kernel, grid=(t // tm,),
            in_specs=[row, row, vec], out_specs=row,
            out_shape=jax.ShapeDtypeStruct((t, d), F32),
            compiler_params=_params("parallel"), name="post_norm_last",
        )(y, h, g_post.reshape(1, d)), None
    return pl.pallas_call(
        _post_kernel, grid=(t // tm,),
        in_specs=[row, row, vec, vec], out_specs=[row, row],
        out_shape=[jax.ShapeDtypeStruct((t, d), F32), jax.ShapeDtypeStruct((t, d), BF16)],
        compiler_params=_params("parallel"), name="post_norm",
    )(y, h, g_post.reshape(1, d), g_next.reshape(1, d))


def _inproj_kernel(u_ref, w_ref, wf_ref, bf_ref, p_ref, lf_ref):
    u = u_ref[...]
    p_ref[...] = jnp.dot(u, w_ref[...], preferred_element_type=F32).astype(p_ref.dtype)

    @pl.when(pl.program_id(1) == 0)
    def _():
        logit = jnp.dot(u, wf_ref[...], preferred_element_type=F32) + bf_ref[...]
        lf_ref[...] = _log_sigmoid(logit)


def _matmul_kernel(u_ref, w_ref, p_ref):
    p_ref[...] = jnp.dot(u_ref[...], w_ref[...], preferred_element_type=F32).astype(p_ref.dtype)


def _inproj_plain(u, w, layer, tm, tn):
    t, d = u.shape
    n = w.shape[2]
    return pl.pallas_call(
        _matmul_kernel,
        grid=(t // tm, n // tn),
        in_specs=[pl.BlockSpec((tm, d), lambda i, j: (i, 0)),
                  pl.BlockSpec((None, d, tn), lambda i, j: (layer, 0, j))],
        out_specs=pl.BlockSpec((tm, tn), lambda i, j: (i, j)),
        out_shape=jax.ShapeDtypeStruct((t, n), BF16),
        compiler_params=_params("parallel", "arbitrary"), name="in_proj_rest",
    )(u, w)


def _inproj(u, w, n, wf, bf, layer, tm, tn):
    t, d = u.shape
    return pl.pallas_call(
        _inproj_kernel,
        grid=(t // tm, n // tn),
        in_specs=[pl.BlockSpec((tm, d), lambda i, j: (i, 0)),
                  pl.BlockSpec((None, d, tn), lambda i, j: (layer, 0, j)),
                  pl.BlockSpec((None, d, HEAD_DIM), lambda i, j: (layer, 0, 0)),
                  pl.BlockSpec((None, 1, HEAD_DIM), lambda i, j: (layer, 0, 0))],
        out_specs=[pl.BlockSpec((tm, tn), lambda i, j: (i, j)),
                   pl.BlockSpec((tm, HEAD_DIM), lambda i, j: (i, 0))],
        out_shape=[jax.ShapeDtypeStruct((t, n), BF16),
                   jax.ShapeDtypeStruct((t, HEAD_DIM), F32)],
        compiler_params=_params("parallel", "arbitrary"), name="in_proj",
    )(u, w, wf, bf)


def _outproj_kernel(a_ref, b_ref, c_ref, d_ref, w_ref, y_ref):
    x = jnp.concatenate([a_ref[...], b_ref[...], c_ref[...], d_ref[...]], axis=1)
    y_ref[...] = jnp.dot(x, w_ref[...], preferred_element_type=F32).astype(y_ref.dtype)


def _outproj(branches, w, layer, tm, tn):
    t, bw = branches[0].shape
    _, k, n = w.shape
    bspec = pl.BlockSpec((tm, bw), lambda i, j: (i, 0))
    return pl.pallas_call(
        _outproj_kernel,
        grid=(t // tm, n // tn),
        in_specs=[bspec, bspec, bspec, bspec, pl.BlockSpec((None, k, tn), lambda i, j: (layer, 0, j))],
        out_specs=pl.BlockSpec((tm, tn), lambda i, j: (i, j)),
        out_shape=jax.ShapeDtypeStruct((t, n), BF16),
        compiler_params=_params("parallel", "arbitrary"), name="out_proj",
    )(*branches, w)


def _split3(x):
    hi = x.astype(BF16).astype(F32)
    r1 = x - hi
    mid = r1.astype(BF16).astype(F32)
    lo = (r1 - mid).astype(BF16).astype(F32)
    return hi, mid, lo


def _cumsum_kernel(lf_ref, tri_ref, qx_ref, kx_ref, carry_ref, *, n_heads):
    @pl.when(pl.program_id(1) == 0)
    def _():
        carry_ref[...] = jnp.zeros_like(carry_ref)

    hi, mid, lo = _split3(lf_ref[0])
    tri = tri_ref[...]
    c = (jnp.dot(tri, hi.astype(BF16), preferred_element_type=F32)
         + jnp.dot(tri, mid.astype(BF16), preferred_element_type=F32)
         + jnp.dot(tri, lo.astype(BF16), preferred_element_type=F32)) + carry_ref[...]
    carry_ref[...] = c[-1:, :]

    c2 = c * LOG2E
    lane = lax.broadcasted_iota(jnp.int32, c2.shape, 1)
    ones = jnp.where(lane < 3, 1.0, 0.0)
    for h in range(n_heads):
        f = jnp.sum(jnp.where(lane == h, c2, 0.0), axis=1, keepdims=True)
        hi, mid, lo = _split3(f)
        pieces = jnp.where(lane == 0, hi, jnp.where(lane == 1, mid, jnp.where(lane == 2, lo, 0.0)))
        qx_ref[0, h] = (pieces + pltpu.roll(ones, 3, 1)).astype(qx_ref.dtype)
        kx_ref[0, h] = (ones - pltpu.roll(pieces, 3, 1)).astype(kx_ref.dtype)


def _forget_columns(lf, n_heads, tc):
    b, l, w = lf.shape
    tri = jnp.tril(jnp.ones((tc, tc), F32)).astype(BF16)
    out = jax.ShapeDtypeStruct((b, n_heads, l, w), BF16)
    ospec = pl.BlockSpec((1, n_heads, tc, w), lambda i, j: (i, 0, j, 0))
    return pl.pallas_call(
        functools.partial(_cumsum_kernel, n_heads=n_heads),
        grid=(b, l // tc),
        in_specs=[pl.BlockSpec((1, tc, w), lambda i, j: (i, j, 0)),
                  pl.BlockSpec((tc, tc), lambda i, j: (0, 0))],
        out_specs=[ospec, ospec],
        out_shape=[out, out],
        scratch_shapes=[pltpu.VMEM((1, w), F32)],
        compiler_params=_params("parallel", "arbitrary"), name="forget_cumsum",
    )(lf, tri)


def _fox_kernel(q_ref, qx_ref, k_ref, kx_ref, v_ref, g_ref, o_ref, s_sc, m_sc, acc_sc, *, tq, nq):
    acc_sc[...] = jnp.zeros_like(acc_sc)
    m_sc[...] = jnp.full_like(m_sc, NEG_BIG)
    nrep = tq // HEAD_DIM
    ones = jnp.ones((tq, HEAD_DIM), BF16)
    contract_last = (((1,), (1,)), ((), ()))
    ahead = lax.broadcasted_iota(jnp.int32, (tq, tq), 1) - lax.broadcasted_iota(jnp.int32, (tq, tq), 0)

    def rows_of(blk):
        return pl.ds(pl.multiple_of(blk * tq, tq), tq)

    def scores(qi, kb, slot):
        q = jnp.concatenate([q_ref[0, rows_of(qi), :], qx_ref[0, 0, rows_of(qi), :]], axis=1)
        k = jnp.concatenate([k_ref[0, rows_of(kb), :], kx_ref[0, 0, rows_of(kb), :]], axis=1)
        s_sc[slot] = lax.dot_general(q, k, contract_last, preferred_element_type=F32)

    def accumulate(qi, kb, slot):
        v = jnp.concatenate([v_ref[0, rows_of(kb), :], ones], axis=1)
        s = jnp.where(ahead <= (qi - kb) * tq, s_sc[slot], NEG_BIG)
        m_prev = jnp.where(kb == 0, NEG_BIG, m_sc[...])
        m_new = jnp.maximum(m_prev, jnp.max(s, axis=1, keepdims=True))
        alpha = jnp.exp2(m_prev - m_new)
        p = jnp.exp2(s - jnp.tile(m_new, (1, nrep)))
        acc_sc[rows_of(qi), :] = (jnp.tile(alpha, (1, 2)) * acc_sc[rows_of(qi), :]
                                  + jnp.dot(p.astype(BF16), v, preferred_element_type=F32))
        m_sc[...] = m_new

    def step(tile, slot, prefetch):
        qi, kb = tile
        last = kb == qi
        nxt = (jnp.where(last, qi + 1, qi), jnp.where(last, 0, kb + 1))
        if prefetch:
            scores(jnp.minimum(nxt[0], nq - 1), nxt[1], 1 - slot)
        accumulate(qi, kb, slot)
        return nxt

    n_tiles = nq * (nq + 1) // 2
    unroll = 8
    scores(0, 0, 0)

    def body(i, tile):
        for j in range(unroll):
            tile = step(tile, j % 2, True)
        return tile

    tile = lax.fori_loop(0, n_tiles // unroll, body, (jnp.int32(0), jnp.int32(0)))
    left = n_tiles % unroll
    for j in range(left):
        tile = step(tile, j % 2, j + 1 < left)

    def finish(blk, carry):
        acc = acc_sc[rows_of(blk), :]
        gate = g_ref[0, rows_of(blk), :].astype(F32)
        o_ref[0, rows_of(blk), :] = (acc[:, :HEAD_DIM] / acc[:, HEAD_DIM:] * _silu(gate)).astype(o_ref.dtype)
        return carry

    lax.fori_loop(0, nq, finish, 0)


def _fox_attention(p3, qx, kx, n_heads, tq):
    b, l, _ = p3.shape
    nq = l // tq
    kern = functools.partial(_fox_kernel, tq=tq, nq=nq)
    col = lambda blk0: pl.BlockSpec((1, l, HEAD_DIM), lambda bi, h: (bi, 0, blk0 + h))
    per_head = pl.BlockSpec((1, 1, l, HEAD_DIM), lambda bi, h: (bi, h, 0, 0))
    return pl.pallas_call(
        kern,
        grid=(b, n_heads),
        in_specs=[col(0), per_head, col(n_heads), per_head, col(2 * n_heads), col(3 * n_heads)],
        out_specs=col(0),
        out_shape=jax.ShapeDtypeStruct((b, l, n_heads * HEAD_DIM), BF16),
        scratch_shapes=[pltpu.VMEM((2, tq, tq), F32), pltpu.VMEM((tq, HEAD_DIM), F32),
                        pltpu.VMEM((l, 2 * HEAD_DIM), F32)],
        compiler_params=_params("parallel", "arbitrary"), name="fox_attn",
    )(p3, qx, p3, kx, p3, p3)


def _sb_kernel(q_ref, k_ref, v_ref, g_ref, tri_ref, o_ref, z_sc, cost_sc, acc_sc, *, tq, tk):
    qi = pl.program_id(2)
    ng = tq // tk
    cost_sc[...] = jnp.zeros_like(cost_sc)
    acc_sc[...] = jnp.zeros_like(acc_sc)
    groups = [pl.ds(r * tk, tk) for r in range(ng)]
    last_step = qi * ng + ng - 1

    def key_rows(r, n):
        kb = jnp.maximum(qi * ng + r - n, 0)
        return pl.ds(pl.multiple_of(kb * tk, tk), tk)

    def logits(n, slot):
        for r, rows in enumerate(groups):
            z_sc[slot, rows, :] = lax.dot_general(q_ref[0, rows, :], k_ref[0, key_rows(r, n), :],
                                                  (((1,), (1,)), ((), ())), preferred_element_type=F32)

    def process(n, slot, diagonal):
        for r, rows in enumerate(groups):
            z = z_sc[slot, rows, :]
            softplus = jnp.maximum(z, 0.0) + jnp.log(1.0 + jnp.exp(-jnp.abs(z)))
            log_beta = z - softplus
            if diagonal:
                mask = (lax.broadcasted_iota(jnp.int32, (tk, tk), 1)
                        < lax.broadcasted_iota(jnp.int32, (tk, tk), 0))
                softplus = jnp.where(mask, softplus, 0.0)
            within = jnp.dot(softplus.astype(BF16), tri_ref[...], preferred_element_type=F32)
            w = jnp.exp(log_beta - within - jnp.tile(cost_sc[rows, :], (1, tk // HEAD_DIM)))
            if diagonal:
                w = jnp.where(mask, w, 0.0)
            pv = jnp.dot(w.astype(BF16), v_ref[0, key_rows(r, n), :], preferred_element_type=F32)
            spent = jnp.sum(softplus, axis=1, keepdims=True)
            if not diagonal:
                has_keys = (qi * ng + r - n >= 0).astype(F32)
                pv = pv * has_keys
                spent = spent * has_keys
            acc_sc[rows, :] += pv
            cost_sc[rows, :] += spent

    logits(0, 0)
    logits(1, 1)
    process(0, 0, True)
    logits(2, 0)
    process(1, 1, False)

    def cond(state):
        return jnp.logical_and(state[0] <= last_step, state[1] < SB_EXP_UNDERFLOW)

    def body(state):
        n = state[0]
        logits(n + 1, 1)
        process(n, 0, False)
        logits(n + 2, 0)
        process(n + 1, 1, False)
        return n + 2, jnp.min(cost_sc[...])

    lax.while_loop(cond, body, (2, jnp.min(cost_sc[...])))
    gate = g_ref[0].astype(F32)
    o_ref[0] = (acc_sc[...] * _silu(gate)).astype(o_ref.dtype)


def _sb_attention(p3, col0, n_heads, tq, tk):
    b, l, _ = p3.shape
    nq = l // tq
    tri = jnp.tril(jnp.ones((tk, tk), F32), k=-1).astype(BF16)
    kern = functools.partial(_sb_kernel, tq=tq, tk=tk)
    return pl.pallas_call(
        kern,
        grid=(b, n_heads, nq),
        in_specs=[pl.BlockSpec((1, tq, HEAD_DIM), lambda bi, h, qi: (bi, qi, col0 + h)),
                  pl.BlockSpec((1, l, HEAD_DIM), lambda bi, h, qi: (bi, 0, col0 + n_heads + h)),
                  pl.BlockSpec((1, l, HEAD_DIM), lambda bi, h, qi: (bi, 0, col0 + 2 * n_heads + h)),
                  pl.BlockSpec((1, tq, HEAD_DIM), lambda bi, h, qi: (bi, qi, col0 + 3 * n_heads + h)),
                  pl.BlockSpec((tk, tk), lambda bi, h, qi: (0, 0))],
        out_specs=pl.BlockSpec((1, tq, HEAD_DIM), lambda bi, h, qi: (bi, qi, h)),
        out_shape=jax.ShapeDtypeStruct((b, l, n_heads * HEAD_DIM), BF16),
        scratch_shapes=[pltpu.VMEM((2, tq, tk), F32), pltpu.VMEM((tq, HEAD_DIM), F32),
                        pltpu.VMEM((tq, HEAD_DIM), F32)],
        compiler_params=_params("parallel", "parallel", "arbitrary"), name="sb_attn",
    )(p3, p3, p3, p3, tri)


def _ssm_weights(a_re, a_im, log_dt, b_re, b_im, c_re, c_im):
    g, p = a_re.shape
    hg = b_re.shape[-1]
    c = SSM_CHUNK
    dt = jnp.exp(log_dt.astype(F32))[:, None]
    a_re = a_re.astype(F32)
    a_im = a_im.astype(F32)
    n = jnp.arange(c + 1, dtype=F32)[:, None, None]
    mag = jnp.exp(a_re * dt * n)
    ang = a_im * dt * n
    pw_re, pw_im = mag * jnp.cos(ang), mag * jnp.sin(ang)
    num_re, num_im = pw_re[1] - 1.0, pw_im[1]
    den = a_re * a_re + a_im * a_im
    f_re = (num_re * a_re + num_im * a_im) / den
    f_im = (num_im * a_re - num_re * a_im) / den
    bb_re = f_re[..., None] * b_re - f_im[..., None] * b_im
    bb_im = f_re[..., None] * b_im + f_im[..., None] * b_re
    cc_re, cc_im = c_re.astype(F32), c_im.astype(F32)
    hi = lax.Precision.HIGHEST
    cl_re = cc_re[None] * pw_re[:, :, None, :] - cc_im[None] * pw_im[:, :, None, :]
    cl_im = cc_re[None] * pw_im[:, :, None, :] + cc_im[None] * pw_re[:, :, None, :]
    cl = jnp.concatenate([cl_re[:c], cl_im[:c]], axis=3).transpose(1, 0, 2, 3)
    kk = jnp.matmul(cl.reshape(g, c * hg, 2 * p), jnp.concatenate([bb_re, -bb_im], axis=1), precision=hi)
    kk = kk.reshape(g, c, hg, hg).transpose(1, 0, 2, 3)
    ug = SSM_UNIT // hg
    nu = g // ug

    def spread(x, inner):
        rows, w = x.shape
        y = jnp.dot(x.astype(BF16), jnp.tile(jnp.eye(w, dtype=BF16), (1, ug)), preferred_element_type=F32)
        row_group = (lax.broadcasted_iota(jnp.int32, y.shape, 0) // inner) % ug
        col_group = lax.broadcasted_iota(jnp.int32, y.shape, 1) // w
        return jnp.where(row_group == col_group, y, 0.0).astype(BF16)

    bd = spread(kk.transpose(0, 1, 3, 2).reshape(c * g * hg, hg), hg).reshape(c, nu, SSM_UNIT, SSM_UNIT)
    zero = jnp.zeros_like(bd[0])
    m = jnp.concatenate([jnp.concatenate([bd[t - s] if t >= s else zero for t in range(c)], axis=2)
                         for s in range(c)], axis=1)
    n_rev = jnp.arange(c - 1, -1, -1, dtype=F32)[:, None, None]
    pr = jnp.exp(a_re * dt * n_rev) * jnp.cos(a_im * dt * n_rev)
    pi_ = jnp.exp(a_re * dt * n_rev) * jnp.sin(a_im * dt * n_rev)
    be_re = pr[..., None] * bb_re[None] - pi_[..., None] * bb_im[None]
    be_im = pr[..., None] * bb_im[None] + pi_[..., None] * bb_re[None]

    def bend(x):
        x = x.reshape(c, nu, ug, p, hg).transpose(1, 0, 2, 4, 3)
        return spread(x.reshape(nu * c * SSM_UNIT, p), hg).reshape(nu, c * SSM_UNIT, ug * p)

    def cout(x):
        x = x.reshape(c, nu, ug, hg, p).transpose(0, 1, 2, 4, 3)
        x = spread(x.reshape(c * nu * ug * p, hg), p).reshape(c, nu, ug * p, SSM_UNIT)
        return jnp.concatenate([x[t] for t in range(c)], axis=2)

    lam_re = pw_re[c].reshape(nu, 1, ug * p)
    lam_im = pw_im[c].reshape(nu, 1, ug * p)
    return m, bend(be_re), bend(be_im), cout(cl_re[1:]), cout(-cl_im[1:]), lam_re, lam_im


def _ssm_kernel(x_ref, m_ref, bre_ref, bim_ref, cre_ref, cim_ref, lre_ref, lim_ref, y_ref,
                xf_sc, zre_sc, zim_sc, yf_sc, *, n_chunk):
    c = SSM_CHUNK
    xf_sc[...] = x_ref[0].astype(F32)
    u = jnp.concatenate([xf_sc[pl.ds(s, n_chunk, stride=c), :] for s in range(c)], axis=1).astype(BF16)
    zre_sc[...] = jnp.dot(u, bre_ref[0], preferred_element_type=F32)
    zim_sc[...] = jnp.dot(u, bim_ref[0], preferred_element_type=F32)
    lre = lre_ref[0]
    lim = lim_ref[0]

    def body(kc, carry):
        xr, xi = carry
        zr = zre_sc[pl.ds(kc, 1), :]
        zi = zim_sc[pl.ds(kc, 1), :]
        zre_sc[pl.ds(kc, 1), :] = xr
        zim_sc[pl.ds(kc, 1), :] = xi
        return lre * xr - lim * xi + zr, lre * xi + lim * xr + zi

    zero = jnp.zeros((1, zre_sc.shape[1]), F32)
    lax.fori_loop(0, n_chunk, body, (zero, zero))
    y = (jnp.dot(u, m_ref[0], preferred_element_type=F32)
         + jnp.dot(zre_sc[...].astype(BF16), cre_ref[0], preferred_element_type=F32)
         + jnp.dot(zim_sc[...].astype(BF16), cim_ref[0], preferred_element_type=F32))
    for t in range(c):
        yf_sc[pl.ds(t, n_chunk, stride=c), :] = y[:, t * SSM_UNIT:(t + 1) * SSM_UNIT]
    y_ref[0] = yf_sc[...].astype(y_ref.dtype)


def _ssm_scan(p3, x_blk0, weights, layer):
    b, l, _ = p3.shape
    m, bre, bim, cre, cim, lre, lim = weights
    nu = m.shape[1]
    lanes = m.shape[2]
    sp = bre.shape[3]
    nck = l // SSM_CHUNK
    kern = functools.partial(_ssm_kernel, n_chunk=nck)
    per_unit = lambda r, cc: pl.BlockSpec((None, 1, r, cc), lambda u, bi: (layer, u, 0, 0))
    return pl.pallas_call(
        kern,
        grid=(nu, b),
        in_specs=[pl.BlockSpec((1, l, SSM_UNIT), lambda u, bi: (bi, 0, x_blk0 + u)),
                  per_unit(lanes, lanes), per_unit(lanes, sp), per_unit(lanes, sp),
                  per_unit(sp, lanes), per_unit(sp, lanes), per_unit(1, sp), per_unit(1, sp)],
        out_specs=pl.BlockSpec((1, l, SSM_UNIT), lambda u, bi: (bi, 0, u)),
        out_shape=jax.ShapeDtypeStruct((b, l, nu * SSM_UNIT), BF16),
        scratch_shapes=[pltpu.VMEM((l, SSM_UNIT), F32), pltpu.VMEM((nck, sp), F32),
                        pltpu.VMEM((nck, sp), F32), pltpu.VMEM((l, SSM_UNIT), F32)],
        compiler_params=_params("parallel", "arbitrary"), name="ssm_scan",
    )(p3, m, bre, bim, cre, cim, lre, lim)


def _glu_kernel(y_ref, u_ref, g_ref, d_ref, w_ref, bg_ref, o_ref):
    y = y_ref[...].astype(F32) + d_ref[...] * u_ref[...].astype(F32)
    y = jax.nn.gelu(y)
    z = jnp.dot(y.astype(BF16), w_ref[...], preferred_element_type=F32) + bg_ref[...]
    o_ref[...] = (y * jax.nn.sigmoid(z) * _silu(g_ref[...].astype(F32))).astype(o_ref.dtype)


def _ssm_glu(y, p2, u_blk, gate_blk, d_skip, w_glu, b_glu, layer, tm):
    t, bw = y.shape
    return pl.pallas_call(
        _glu_kernel,
        grid=(t // tm,),
        in_specs=[pl.BlockSpec((tm, bw), lambda i: (i, 0)),
                  pl.BlockSpec((tm, bw), lambda i: (i, u_blk)),
                  pl.BlockSpec((tm, bw), lambda i: (i, gate_blk)),
                  pl.BlockSpec((1, bw), lambda i: (0, 0)),
                  pl.BlockSpec((None, bw, bw), lambda i: (layer, 0, 0)),
                  pl.BlockSpec((1, bw), lambda i: (0, 0))],
        out_specs=pl.BlockSpec((tm, bw), lambda i: (i, 0)),
        out_shape=jax.ShapeDtypeStruct((t, bw), BF16),
        compiler_params=_params("parallel"), name="ssm_glu",
    )(y, p2, p2, d_skip.reshape(1, bw).astype(F32), w_glu, b_glu.reshape(1, bw).astype(F32))


def _lru_kernel(x_ref, g_ref, cw_ref, cb_ref, wg_ref, br_ref, bi_ref, lam_ref, o_ref,
                tail_sc, h_sc, a_sc, b_sc, *, tl):
    @pl.when(pl.program_id(1) == 0)
    def _():
        tail_sc[...] = jnp.zeros_like(tail_sc)
        h_sc[...] = jnp.zeros_like(h_sc)

    x = x_ref[0].astype(F32)
    w = x.shape[1]
    xcat = jnp.concatenate([tail_sc[...], x], axis=0)
    xc = cb_ref[...] + cw_ref[CONV_WIDTH - 1:CONV_WIDTH, :] * x
    for d in range(1, CONV_WIDTH):
        xd = pltpu.roll(xcat, d, 0)[SUBLANES:, :]
        xc = xc + cw_ref[CONV_WIDTH - 1 - d:CONV_WIDTH - d, :] * xd
    tail_sc[...] = x[tl - SUBLANES:, :]

    xb = xc.astype(BF16)
    nblk = w // MXU_DIM
    gates = [jnp.dot(xb[:, kb * MXU_DIM:(kb + 1) * MXU_DIM], wg_ref[kb], preferred_element_type=F32)
             for kb in range(nblk)]
    gr = jnp.concatenate([gt[:, :MXU_DIM] for gt in gates], axis=1)
    gi = jnp.concatenate([gt[:, MXU_DIM:] for gt in gates], axis=1)
    r = jax.nn.sigmoid(gr + br_ref[...])
    i = jax.nn.sigmoid(gi + bi_ref[...])
    log_a = r * (LRU_C * _log_sigmoid(lam_ref[...]))
    a = jnp.exp(log_a)
    gap = 1.0 - a * a
    bx = gap * lax.rsqrt(jnp.maximum(gap, TINY)) * (i * xc)

    nt = tl // SUBLANES
    a = a.reshape(nt, SUBLANES, w)
    bx = bx.reshape(nt, SUBLANES, w)
    sub = lax.broadcasted_iota(jnp.int32, (nt, SUBLANES, w), 1)
    d = 1
    while d < SUBLANES:
        a_prev = pltpu.roll(a, d, 1)
        b_prev = pltpu.roll(bx, d, 1)
        use = sub >= d
        bx = jnp.where(use, a * b_prev + bx, bx)
        a = jnp.where(use, a * a_prev, a)
        d *= 2
    a_sc[...] = a.reshape(tl, w)
    b_sc[...] = bx.reshape(tl, w)

    def body(kt, h_last):
        s = pl.multiple_of(kt * SUBLANES, SUBLANES)
        ht = a_sc[pl.ds(s, SUBLANES), :] * h_last + b_sc[pl.ds(s, SUBLANES), :]
        b_sc[pl.ds(s, SUBLANES), :] = ht
        return ht[SUBLANES - 1:, :]

    h_sc[...] = lax.fori_loop(0, tl // SUBLANES, body, h_sc[...])
    o_ref[0] = (b_sc[...] * _silu(g_ref[0].astype(F32))).astype(o_ref.dtype)


def _lru_gate_tiles(w_r, w_i):
    depth, nb = w_r.shape[:2]
    per = MXU_DIM // LRU_BLOCK
    nblk = nb // per

    def tile_diag(wg):
        wg = wg.astype(F32).reshape(depth, nblk, per, LRU_BLOCK, LRU_BLOCK)
        eye = jnp.eye(per, dtype=F32)
        return jnp.einsum('dkaio,ab->dkaibo', wg, eye).reshape(depth, nblk, MXU_DIM, MXU_DIM)

    return jnp.concatenate([tile_diag(w_r), tile_diag(w_i)], axis=3).astype(BF16)


def _lru(p3, x_blk, gate_blk, conv_w, conv_b, wg, b_r, b_i, lam, layer, tl):
    b, l, _ = p3.shape
    bw = conv_w.shape[1]
    nblk = bw // MXU_DIM
    vec = lambda a: a.reshape(1, bw).astype(F32)
    kern = functools.partial(_lru_kernel, tl=tl)
    full = lambda shape: pl.BlockSpec(shape, lambda bi, ti: (0,) * len(shape))
    return pl.pallas_call(
        kern,
        grid=(b, l // tl),
        in_specs=[pl.BlockSpec((1, tl, bw), lambda bi, ti: (bi, ti, x_blk)),
                  pl.BlockSpec((1, tl, bw), lambda bi, ti: (bi, ti, gate_blk)),
                  full((CONV_WIDTH, bw)), full((1, bw)),
                  pl.BlockSpec((None, nblk, MXU_DIM, 2 * MXU_DIM), lambda bi, ti: (layer, 0, 0, 0)),
                  full((1, bw)), full((1, bw)), full((1, bw))],
        out_specs=pl.BlockSpec((1, tl, bw), lambda bi, ti: (bi, ti, 0)),
        out_shape=jax.ShapeDtypeStruct((b, l, bw), BF16),
        scratch_shapes=[pltpu.VMEM((SUBLANES, bw), F32), pltpu.VMEM((1, bw), F32),
                        pltpu.VMEM((tl, bw), F32), pltpu.VMEM((tl, bw), F32)],
        compiler_params=_params("parallel", "arbitrary"), name="rg_lru",
    )(p3, p3, conv_w.astype(F32), vec(conv_b), wg, vec(b_r), vec(b_i), vec(lam))


def _tiles(b, l):
    t = b * l
    return dict(
        norm=min(256, t),
        mm_m=min(1024, t),
        mm_n=1024,
        cum=min(512, l),
        fox=min(512, l),
        sb_q=min(1024, l),
        sb_k=min(MXU_DIM, l),
        glu=min(512, t),
        lru=min(512, l),
    )


def kernel(x, g_pre, w_in, b_forget, ssm_a_re, ssm_a_im, ssm_log_dt, ssm_b_re, ssm_b_im, ssm_c_re, ssm_c_im, ssm_d, w_glu, b_glu, conv_w, conv_b, w_rgate, b_rgate, w_igate, b_igate, lru_lambda, w_out, g_post):
    b, l, d = x.shape
    depth = w_in.shape[0]
    bw = d // 4
    nh = bw // HEAD_DIM
    t = b * l
    ts = _tiles(b, l)
    assert w_in.shape[2] == 12 * bw + nh and l % SSM_CHUNK == 0

    def scaled_q(w, nblk, scale):
        sc = jnp.ones((nblk,), F32).at[0].set(scale)
        return (w.reshape(depth, d, nblk, bw) * sc[None, None, :, None]).reshape(depth, d, nblk * bw).astype(BF16)

    w_fox = scaled_q(w_in[:, :, :4 * bw], 4, HEAD_DIM ** -0.5 * LOG2E)
    w_rest = scaled_q(w_in[:, :, 4 * bw + nh:], 8, HEAD_DIM ** -0.5)
    w_f = jnp.pad(w_in[:, :, 4 * bw:4 * bw + nh], ((0, 0), (0, 0), (0, HEAD_DIM - nh))).astype(BF16)
    b_f = jnp.pad(b_forget.astype(F32), ((0, 0), (0, HEAD_DIM - nh))).reshape(depth, 1, HEAD_DIM)
    ssm_w = jax.vmap(_ssm_weights)(ssm_a_re, ssm_a_im, ssm_log_dt, ssm_b_re, ssm_b_im, ssm_c_re, ssm_c_im)
    lru_wg = _lru_gate_tiles(w_rgate, w_igate)
    w_glu_b = w_glu.astype(BF16)
    w_out_b = w_out.astype(BF16)

    h = x.reshape(t, d)
    u = _rmsnorm(h, g_pre[0], ts["norm"])
    for layer in range(depth):
        pf, log_f = _inproj(u, w_fox, 4 * bw, w_f, b_f, layer, ts["mm_m"], ts["mm_n"])
        pr = _inproj_plain(u, w_rest, layer, ts["mm_m"], ts["mm_n"])
        pf3 = pf.reshape(b, l, 4 * bw)
        pr3 = pr.reshape(b, l, 8 * bw)

        qx, kx = _forget_columns(log_f.reshape(b, l, HEAD_DIM), nh, ts["cum"])
        o_fox = _fox_attention(pf3, qx, kx, nh, ts["fox"])
        o_sb = _sb_attention(pr3, 0, nh, ts["sb_q"], ts["sb_k"])

        y_ssm = _ssm_scan(pr3, 4 * bw // SSM_UNIT, ssm_w, layer)
        o_ssm = _ssm_glu(y_ssm.reshape(t, bw), pr, 4, 5, ssm_d[layer], w_glu_b, b_glu[layer], layer, ts["glu"])

        o_lru = _lru(pr3, 6, 7, conv_w[layer], conv_b[layer], lru_wg, b_rgate[layer], b_igate[layer],
                     lru_lambda[layer], layer, ts["lru"])

        y = _outproj([o_fox.reshape(t, bw), o_sb.reshape(t, bw), o_ssm, o_lru.reshape(t, bw)],
                     w_out_b, layer, ts["mm_m"], ts["mm_n"])
        g_next = g_pre[layer + 1] if layer + 1 < depth else None
        h, u = _post(y, h, g_post[layer], g_next, ts["norm"])
    return h.reshape(b, l, d)
```
